```python
import jax
import jax.numpy as jnp
from jax import lax
import numpy as np

D_MODEL = 2048
BATCH = 4
SEQ = 2048
DEPTH = 2

CTX_LEN = 256
GRID_W = 64

N_GLA_HEADS = 4
GLA_KEY_DIM = D_MODEL // 2
GLA_VAL_DIM = D_MODEL
HEAD_K = GLA_KEY_DIM // N_GLA_HEADS
HEAD_V = GLA_VAL_DIM // N_GLA_HEADS
GATE_RANK = 16
GATE_TAU = 16.0
GLA_CHUNK = 64

CONV_DIM = D_MODEL
CONV_K = 3

D_FF = 5632
N_EXPERTS = 8
TOP_K = 2
N_DENSE = (DEPTH + 1) // 2
N_MOE = DEPTH // 2

EPS = 1e-6

OFF_Q = 0
OFF_K = OFF_Q + GLA_KEY_DIM
OFF_V = OFF_K + GLA_KEY_DIM
OFF_AF = OFF_V + GLA_VAL_DIM
OFF_AB = OFF_AF + GATE_RANK
CTX_COLS = OFF_AB + GATE_RANK
OFF_R = CTX_COLS
OFF_CB = OFF_R + GLA_VAL_DIM
OFF_CC = OFF_CB + CONV_DIM
OFF_CX = OFF_CC + CONV_DIM
OFF_GA = OFF_CX + CONV_DIM
OFF_GB = OFF_GA + D_MODEL
N_IN_COLS = OFF_GB + D_MODEL

kernel_name = 'hybrid_conv_gla_moe_dit'


def rmsnorm(x, g):
    xf = x.astype(jnp.float32)
    xf = xf * lax.rsqrt(jnp.mean(xf * xf, axis=-1, keepdims=True) + EPS)
    return (xf * g.astype(jnp.float32)).astype(x.dtype)


def modulate(h, shift, scale):
    return h * (1 + scale) + shift


def gla_inputs(p, w2_f, b_f, w2_b, b_b):
    lead = p.shape[:-1]

    def heads(t, hd):
        return t.reshape(*lead, -1, hd).astype(jnp.float32)

    q = heads(p[..., OFF_Q:OFF_K], HEAD_K) * (HEAD_K ** -0.5)
    k = heads(p[..., OFF_K:OFF_V], HEAD_K)
    v = heads(p[..., OFF_V:OFF_AF], HEAD_V)
    la_f = heads(jax.nn.log_sigmoid((p[..., OFF_AF:OFF_AB] @ w2_f + b_f).astype(jnp.float32)) / GATE_TAU, HEAD_K)
    la_b = heads(jax.nn.log_sigmoid((p[..., OFF_AB:CTX_COLS] @ w2_b + b_b).astype(jnp.float32)) / GATE_TAU, HEAD_K)
    return q, k, v, la_f, la_b


def gla_chunked(q, k, v, log_a, s0):
    bsz, length = q.shape[:2]
    n_chunks = length // GLA_CHUNK

    def to_chunks(t):
        return jnp.moveaxis(t.reshape(bsz, n_chunks, GLA_CHUNK, *t.shape[2:]), 1, 0)

    mask = jnp.tril(jnp.ones((GLA_CHUNK, GLA_CHUNK), dtype=bool))

    def step(s, inp):
        qc, kc, vc, ac = inp
        b = jnp.cumsum(ac, axis=1)
        b_last = b[:, -1]
        q_t = qc * jnp.exp(b)
        k_t = kc * jnp.exp(-b)
        att = jnp.where(mask, jnp.einsum('bchk,bshk->bhcs', q_t, k_t), 0.0)
        o = jnp.einsum('bhcs,bshv->bchv', att, vc) + jnp.einsum('bchk,bhkv->bchv', q_t, s)
        k_end = kc * jnp.exp(b_last[:, None] - b)
        s_new = jnp.exp(b_last)[..., None] * s + jnp.einsum('bchk,bchv->bhkv', k_end, vc)
        return s_new, o

    s_fin, o = lax.scan(step, s0, (to_chunks(q), to_chunks(k), to_chunks(v), to_chunks(log_a)))
    o = jnp.moveaxis(o, 0, 1).reshape(bsz, length, q.shape[2], v.shape[-1])
    return o, s_fin


def gla_bidirectional(q, k, v, la_f, la_b, s_f0, s_b0):
    o_f, s_f = gla_chunked(q, k, v, la_f, s_f0)

    def flip(t):
        return jnp.flip(t, axis=1)

    o_b, s_b = gla_chunked(flip(q), flip(k), flip(v), flip(la_b), s_b0)
    return o_f + flip(o_b), s_f, s_b


def dwconv_centred(u, w, b):
    pad = CONV_K // 2
    length = u.shape[-2]
    up = jnp.pad(u, [(0, 0)] * (u.ndim - 2) + [(pad, pad), (0, 0)])
    acc = b
    for j in range(CONV_K):
        acc = acc + up[..., j:j + length, :] * w[j]
    return acc


def merge_branches(p, o_gla, norm_g, cw, cb, w_a, w_b, w_o, rows):
    lead = p.shape[:-1]
    o = rmsnorm(o_gla, norm_g.reshape(N_GLA_HEADS, HEAD_V)).reshape(*lead, GLA_VAL_DIM).astype(p.dtype)
    y_b = (o * jax.nn.silu(p[..., OFF_R:OFF_CB])) @ w_b
    u = p[..., OFF_CC:OFF_CX] * p[..., OFF_CX:OFF_GA]
    if rows is None:
        conv = dwconv_centred(u, cw, cb)
    else:
        conv = dwconv_centred(u.reshape(lead[0], rows, GRID_W, CONV_DIM), cw, cb).reshape(u.shape)
    y_a = (p[..., OFF_CB:OFF_CC] * conv) @ w_a
    merged = jax.nn.sigmoid(p[..., OFF_GA:OFF_GB]) * y_a + jax.nn.sigmoid(p[..., OFF_GB:N_IN_COLS]) * y_b
    return merged @ w_o


def swiglu(h, w1, w3, w2):
    return (jax.nn.silu(h @ w1) * (h @ w3)) @ w2


def moe_swiglu(h, w_router, w1, w3, w2):
    flat = h.reshape(-1, h.shape[-1])
    logits = (flat @ w_router).astype(jnp.float32)
    top_v, top_i = lax.top_k(logits, TOP_K)
    probs = jax.nn.softmax(top_v, axis=-1)
    combine = jnp.sum(jax.nn.one_hot(top_i, N_EXPERTS, dtype=jnp.float32) * probs[..., None], axis=1)
    combine = combine.astype(h.dtype)
    out = jnp.zeros_like(flat)
    for e in range(N_EXPERTS):
        out = out + combine[:, e:e + 1] * swiglu(flat, w1[e], w3[e], w2[e])
    return out.reshape(h.shape)


def channel_mixer(h, layer, ffn_w1, ffn_w3, ffn_w2, router_w, moe_w1, moe_w3, moe_w2):
    i = layer // 2
    if layer % 2 == 0:
        return swiglu(h, ffn_w1[i], ffn_w3[i], ffn_w2[i])
    return moe_swiglu(h, router_w[i], moe_w1[i], moe_w3[i], moe_w2[i])


def setup_inputs(seed: int = 0) -> dict:
    key = jax.random.key(seed)
    ks = jax.random.split(key, 28)
    D = D_MODEL

    def nrm(k, shape, scale):
        return jax.random.normal(k, shape, jnp.float32) * scale

    def gain(k, shape):
        return 1.0 + 0.1 * jax.random.normal(k, shape, jnp.float32)

    return {
        'x': nrm(ks[0], (BATCH, SEQ, D), 1.0),
        'c': nrm(ks[1], (BATCH, D), 1.0),
        'ctx': nrm(ks[2], (BATCH, CTX_LEN, D), 1.0),
        'c_ctx': nrm(ks[3], (D,), 1.0),
        'w_ada': nrm(ks[4], (DEPTH, D, 6 * D), 0.5 * D ** -0.5),
        'b_ada': nrm(ks[5], (DEPTH, 6 * D), 0.05),
        'pre_mix_g': gain(ks[6], (DEPTH, D)),
        'post_mix_g': gain(ks[7], (DEPTH, D)),
        'pre_ffn_g': gain(ks[8], (DEPTH, D)),
        'post_ffn_g': gain(ks[9], (DEPTH, D)),
        'w_in': nrm(ks[10], (DEPTH, D, N_IN_COLS), D ** -0.5),
        'gate_w2_f': nrm(ks[11], (DEPTH, GATE_RANK, GLA_KEY_DIM), GATE_RANK ** -0.5),
        'gate_b_f': 1.0 + nrm(ks[12], (DEPTH, GLA_KEY_DIM), 0.5),
        'gate_w2_b': nrm(ks[13], (DEPTH, GATE_RANK, GLA_KEY_DIM), GATE_RANK ** -0.5),
        'gate_b_b': 1.0 + nrm(ks[14], (DEPTH, GLA_KEY_DIM), 0.5),
        'gla_norm_g': gain(ks[15], (DEPTH, GLA_VAL_DIM)),
        'conv_w': nrm(ks[16], (DEPTH, CONV_K, CONV_DIM), CONV_K ** -0.5),
        'conv_b': nrm(ks[17], (DEPTH, CONV_DIM), 0.02),
        'w_proj_a': nrm(ks[18], (DEPTH, CONV_DIM, D), CONV_DIM ** -0.5),
        'w_proj_b': nrm(ks[19], (DEPTH, GLA_VAL_DIM, D), GLA_VAL_DIM ** -0.5),
        'w_out': nrm(ks[20], (DEPTH, D, D), D ** -0.5),
        'ffn_w1': nrm(ks[21], (N_DENSE, D, D_FF), D ** -0.5),
        'ffn_w3': nrm(ks[22], (N_DENSE, D, D_FF), D ** -0.5),
        'ffn_w2': nrm(ks[23], (N_DENSE, D_FF, D), D_FF ** -0.5),
        'router_w': nrm(ks[24], (N_MOE, D, N_EXPERTS), D ** -0.5),
        'moe_w1': nrm(ks[25], (N_MOE, N_EXPERTS, D, D_FF), D ** -0.5),
        'moe_w3': nrm(ks[26], (N_MOE, N_EXPERTS, D, D_FF), D ** -0.5),
        'moe_w2': nrm(ks[27], (N_MOE, N_EXPERTS, D_FF, D), D_FF ** -0.5),
    }


def reference(x, c, ctx, c_ctx, w_ada, b_ada, pre_mix_g, post_mix_g, pre_ffn_g, post_ffn_g,
              w_in, gate_w2_f, gate_b_f, gate_w2_b, gate_b_b, gla_norm_g, conv_w, conv_b,
              w_proj_a, w_proj_b, w_out, ffn_w1, ffn_w3, ffn_w2, router_w, moe_w1, moe_w3, moe_w2):
    bsz = x.shape[0]
    rows = x.shape[1] // GRID_W
    xc = ctx
    for layer in range(DEPTH):
        last = layer == DEPTH - 1
        mod = jax.nn.silu(c) @ w_ada[layer] + b_ada[layer]
        sh1, sc1, g1, sh2, sc2, g2 = jnp.split(mod[:, None, :], 6, axis=-1)
        mod_c = jax.nn.silu(c_ctx) @ w_ada[layer] + b_ada[layer]
        csh1, csc1, cg1, csh2, csc2, cg2 = jnp.split(mod_c, 6, axis=-1)
        gate_p = (gate_w2_f[layer], gate_b_f[layer], gate_w2_b[layer], gate_b_b[layer])
        mix_p = (gla_norm_g[layer], conv_w[layer], conv_b[layer],
                 w_proj_a[layer], w_proj_b[layer], w_out[layer])
        w_in_l = w_in[layer]

        hc = modulate(rmsnorm(xc, pre_mix_g[layer]), csh1, csc1)
        pc = hc @ (w_in_l[:, :CTX_COLS] if last else w_in_l)
        s0 = jnp.zeros((bsz, N_GLA_HEADS, HEAD_K, HEAD_V), jnp.float32)
        o_c, s_f, s_b = gla_bidirectional(*gla_inputs(pc, *gate_p), s0, s0)

        hx = modulate(rmsnorm(x, pre_mix_g[layer]), sh1, sc1)
        px = hx @ w_in_l
        o_x, _, _ = gla_bidirectional(*gla_inputs(px, *gate_p), s_f, s_b)
        m_x = merge_branches(px, o_x.astype(jnp.float32), *mix_p, rows=rows)
        x = x + g1 * rmsnorm(m_x, post_mix_g[layer])

        h2 = modulate(rmsnorm(x, pre_ffn_g[layer]), sh2, sc2)
        f_x = channel_mixer(h2, layer, ffn_w1, ffn_w3, ffn_w2, router_w, moe_w1, moe_w3, moe_w2)
        x = x + g2 * rmsnorm(f_x, post_ffn_g[layer])

        if not last:
            m_c = merge_branches(pc, o_c, *mix_p, rows=None)
            xc = xc + cg1 * rmsnorm(m_c, post_mix_g[layer])
            h2c = modulate(rmsnorm(xc, pre_ffn_g[layer]), csh2, csc2)
            f_c = channel_mixer(h2c, layer, ffn_w1, ffn_w3, ffn_w2, router_w, moe_w1, moe_w3, moe_w2)
            xc = xc + cg2 * rmsnorm(f_c, post_ffn_g[layer])
    return x
```

```python
import functools

import jax
import jax.numpy as jnp
from jax import lax
from jax.experimental import pallas as pl
from jax.experimental.pallas import tpu as pltpu

F32 = jnp.float32
BF16 = jnp.bfloat16

EPS = 1e-6
N_HEADS = 4
GATE_RANK = 16
GATE_TAU = 16.0
GLA_CHUNK = 64
GRID_W = 64
TOP_K = 2

LANES = 128
VMEM_LIMIT = 56 * 1024 * 1024

SH1, SC1, G1, SH2, SC2, G2 = range(6)


def _params(*sem):
    return pltpu.CompilerParams(dimension_semantics=sem, vmem_limit_bytes=VMEM_LIMIT)


def _dot(a, b):
    return jnp.dot(a, b, preferred_element_type=F32)


def _dot_nt(a, b):
    return lax.dot_general(a, b, (((1,), (1,)), ((), ())), preferred_element_type=F32)


def _dot_tn(a, b):
    return lax.dot_general(a, b, (((0,), (0,)), ((), ())), preferred_element_type=F32)


def _silu(x):
    return x * jax.nn.sigmoid(x)


def _rms(x):
    return x * lax.rsqrt(jnp.mean(x * x, axis=-1, keepdims=True) + EPS)


def _ada_kernel(c_ref, w_ref, b_ref, o_ref):
    s = _silu(c_ref[...]).astype(BF16)
    o_ref[0] = _dot(s, w_ref[0].astype(BF16)) + b_ref[0]


def _ada(cc, w_ada, b_ada, tn=1024):
    depth, d, n = w_ada.shape
    rows = cc.shape[0]
    return pl.pallas_call(
        _ada_kernel,
        grid=(depth, n // tn),
        in_specs=[pl.BlockSpec((rows, d), lambda l, j: (0, 0)),
                  pl.BlockSpec((1, d, tn), lambda l, j: (l, 0, j)),
                  pl.BlockSpec((1, 1, tn), lambda l, j: (l, 0, j))],
        out_specs=pl.BlockSpec((1, rows, tn), lambda l, j: (l, 0, j)),
        out_shape=jax.ShapeDtypeStruct((depth, rows, n), F32),
        compiler_params=_params("parallel", "parallel"),
        name="ada",
    )(cc, w_ada, b_ada.reshape(depth, 1, n))


def _modulated_norm(x, g, mod_ref, shift_row, scale_row):
    h = _rms(x) * g
    return h * (1.0 + mod_ref[0, scale_row:scale_row + 1, :]) + mod_ref[0, shift_row:shift_row + 1, :]


def _prenorm_kernel(x_ref, g_ref, mod_ref, o_ref):
    o_ref[0] = _modulated_norm(x_ref[0], g_ref[...], mod_ref, SH1, SC1).astype(o_ref.dtype)


def _prenorm(x, g, mod, tt=256):
    b, t, d = x.shape
    return pl.pallas_call(
        _prenorm_kernel,
        grid=(b, t // tt),
        in_specs=[pl.BlockSpec((1, tt, d), lambda i, j: (i, j, 0)),
                  pl.BlockSpec((1, d), lambda i, j: (0, 0)),
                  pl.BlockSpec((1, 6, d), lambda i, j: (i, 0, 0))],
        out_specs=pl.BlockSpec((1, tt, d), lambda i, j: (i, j, 0)),
        out_shape=jax.ShapeDtypeStruct((b, t, d), BF16),
        compiler_params=_params("parallel", "parallel"),
        name="prenorm",
    )(x, g.reshape(1, d), mod)


def _mm_kernel(h_ref, w_ref, o_ref):
    o_ref[...] = _dot(h_ref[...], w_ref[...])


def _matmul(h, w, ncols, tm=1024, tn=1024):
    n, k = h.shape
    tm = min(tm, n)
    return pl.pallas_call(
        _mm_kernel,
        grid=(n // tm, ncols // tn),
        in_specs=[pl.BlockSpec((tm, k), lambda i, j: (i, 0)),
                  pl.BlockSpec((k, tn), lambda i, j: (0, j))],
        out_specs=pl.BlockSpec((tm, tn), lambda i, j: (i, j)),
        out_shape=jax.ShapeDtypeStruct((n, ncols), F32),
        compiler_params=_params("parallel", "parallel"),
        name="in_proj",
    )(h, w)


def _decay_kernel(h_ref, wlr_ref, w2_ref, b_ref, o_ref):
    lr = _dot(h_ref[...], wlr_ref[...]).astype(BF16)
    z = _dot(lr, w2_ref[...]) + b_ref[...]
    log_sig = jnp.minimum(z, 0.0) - jnp.log1p(jnp.exp(-jnp.abs(z)))
    o_ref[...] = log_sig / GATE_TAU


def _decay(h, w_lr, w2, bias, tm=512):
    n, k = h.shape
    r = w_lr.shape[1]
    m = w2.shape[1]
    return pl.pallas_call(
        _decay_kernel,
        grid=(n // tm,),
        in_specs=[pl.BlockSpec((tm, k), lambda i: (i, 0)),
                  pl.BlockSpec((k, r), lambda i: (0, 0)),
                  pl.BlockSpec((r, m), lambda i: (0, 0)),
                  pl.BlockSpec((1, m), lambda i: (0, 0))],
        out_specs=pl.BlockSpec((tm, m), lambda i: (i, 0)),
        out_shape=jax.ShapeDtypeStruct((n, m), F32),
        compiler_params=_params("parallel"),
        name="decay",
    )(h, w_lr, w2, bias)


def _gla_kernel(*refs, n_chunks, scale, want_out):
    if want_out:
        (q_ref, k_ref, v_ref, laf_ref, lab_ref, s0f_ref, s0b_ref, r_ref, g_ref,
         o_ref, sf_ref, sb_ref, of_scr, ob_scr, stf_scr, stb_scr) = refs
    else:
        (q_ref, k_ref, v_ref, laf_ref, lab_ref, s0f_ref, s0b_ref,
         sf_ref, sb_ref, stf_scr, stb_scr) = refs
        of_scr = ob_scr = None
    c_len = GLA_CHUNK
    row = lax.broadcasted_iota(jnp.int32, (c_len, c_len), 0)
    col = lax.broadcasted_iota(jnp.int32, (c_len, c_len), 1)
    mask_f = row >= col
    mask_b = row <= col
    tri_f = mask_f.astype(BF16)
    tri_b = mask_b.astype(BF16)

    stf_scr[...] = s0f_ref[0, 0]
    stb_scr[...] = s0b_ref[0, 0]

    def one_dir(c, la_ref, tri, mask, st_scr, o_scr, last_row):
        sl = pl.ds(pl.multiple_of(c * c_len, c_len), c_len)
        la = la_ref[0, sl, :]
        la_hi = la.astype(BF16)
        la_lo = (la - la_hi.astype(F32)).astype(BF16)
        b = _dot(tri, la_hi) + _dot(tri, la_lo)
        b_last = b[last_row:last_row + 1, :]
        q = q_ref[0, sl, :]
        k = k_ref[0, sl, :]
        v = v_ref[0, sl, :].astype(BF16)
        qt = (q * scale * jnp.exp(b)).astype(BF16)
        kt = (k * jnp.exp(-b)).astype(BF16)
        att = jnp.where(mask, _dot_nt(qt, kt), 0.0).astype(BF16)
        st = st_scr[...]
        if o_scr is not None:
            o_scr[sl, :] = _dot(att, v) + _dot_nt(qt, st.astype(BF16))
        k_end = (k * jnp.exp(b_last - b)).astype(BF16)
        st_scr[...] = jnp.exp(b_last) * st + _dot_tn(v, k_end)

    def body(c, carry):
        one_dir(c, laf_ref, tri_f, mask_f, stf_scr, of_scr, c_len - 1)
        one_dir(n_chunks - 1 - c, lab_ref, tri_b, mask_b, stb_scr, ob_scr, 0)
        return carry

    lax.fori_loop(0, n_chunks, body, 0)
    sf_ref[0, 0] = stf_scr[...]
    sb_ref[0, 0] = stb_scr[...]

    if want_out:
        def epilogue(c, carry):
            sl = pl.ds(pl.multiple_of(c * c_len, c_len), c_len)
            o = _rms(of_scr[sl, :] + ob_scr[sl, :]) * g_ref[...]
            o_ref[0, sl, :] = (o * _silu(r_ref[0, sl, :])).astype(o_ref.dtype)
            return carry

        lax.fori_loop(0, n_chunks, epilogue, 0)


def _gla(p, la, s0f, s0b, gla_g, dk_total, dv_total, off_r, want_out):
    bsz, t, _ = p.shape
    hk = dk_total // N_HEADS
    hv = dv_total // N_HEADS
    nkb = dk_total // hk
    kern = functools.partial(_gla_kernel, n_chunks=t // GLA_CHUNK, scale=hk ** -0.5, want_out=want_out)
    in_specs = [
        pl.BlockSpec((1, t, hk), lambda b, h: (b, 0, h)),
        pl.BlockSpec((1, t, hk), lambda b, h: (b, 0, nkb + h)),
        pl.BlockSpec((1, t, hv), lambda b, h: (b, 0, 2 * dk_total // hv + h)),
        pl.BlockSpec((1, t, hk), lambda b, h: (b, 0, h)),
        pl.BlockSpec((1, t, hk), lambda b, h: (b, 0, nkb + h)),
        pl.BlockSpec((1, 1, hv, hk), lambda b, h: (b, h, 0, 0)),
        pl.BlockSpec((1, 1, hv, hk), lambda b, h: (b, h, 0, 0)),
    ]
    args = [p, p, p, la, la, s0f, s0b]
    st_shape = jax.ShapeDtypeStruct((bsz, N_HEADS, hv, hk), F32)
    st_spec = pl.BlockSpec((1, 1, hv, hk), lambda b, h: (b, h, 0, 0))
    scratch = [pltpu.VMEM((hv, hk), F32), pltpu.VMEM((hv, hk), F32)]
    if want_out:
        in_specs += [pl.BlockSpec((1, t, hv), lambda b, h: (b, 0, off_r // hv + h)),
                     pl.BlockSpec((1, hv), lambda b, h: (0, h))]
        args += [p, gla_g.reshape(1, dv_total)]
        out_shape = (jax.ShapeDtypeStruct((bsz, t, dv_total), BF16), st_shape, st_shape)
        out_specs = (pl.BlockSpec((1, t, hv), lambda b, h: (b, 0, h)), st_spec, st_spec)
        scratch = [pltpu.VMEM((t, hv), F32), pltpu.VMEM((t, hv), F32)] + scratch
    else:
        out_shape = (st_shape, st_shape)
        out_specs = (st_spec, st_spec)
    return pl.pallas_call(
        kern,
        grid=(bsz, N_HEADS),
        in_specs=in_specs,
        out_specs=out_specs,
        out_shape=out_shape,
        scratch_shapes=scratch,
        compiler_params=_params("parallel", "parallel"),
        name="gla",
    )(*args)


def _conv_kernel(cb_ref, cc_ref, cx_ref, w_ref, b_ref, o_ref, *, seg):
    u = cc_ref[0] * cx_ref[0]
    tt = u.shape[0]
    pos = lax.broadcasted_iota(jnp.int32, u.shape, 0) % seg
    u_prev = jnp.where(pos == 0, 0.0, pltpu.roll(u, 1, axis=0))
    u_next = jnp.where(pos == seg - 1, 0.0, pltpu.roll(u, tt - 1, axis=0))
    conv = b_ref[...] + u_prev * w_ref[0:1, :]
    conv = conv + u * w_ref[1:2, :]
    conv = conv + u_next * w_ref[2:3, :]
    o_ref[0] = (cb_ref[0] * conv).astype(o_ref.dtype)


def _gated_conv(p, conv_w, conv_b, d, off_cb, seg, tt=256):
    bsz, t, _ = p.shape
    cb = off_cb // d
    kern = functools.partial(_conv_kernel, seg=seg)
    return pl.pallas_call(
        kern,
        grid=(bsz, t // tt),
        in_specs=[pl.BlockSpec((1, tt, d), lambda b, i: (b, i, cb)),
                  pl.BlockSpec((1, tt, d), lambda b, i: (b, i, cb + 1)),
                  pl.BlockSpec((1, tt, d), lambda b, i: (b, i, cb + 2)),
                  pl.BlockSpec((3, d), lambda b, i: (0, 0)),
                  pl.BlockSpec((1, d), lambda b, i: (0, 0))],
        out_specs=pl.BlockSpec((1, tt, d), lambda b, i: (b, i, 0)),
        out_shape=jax.ShapeDtypeStruct((bsz, t, d), BF16),
        compiler_params=_params("parallel", "parallel"),
        name="gated_conv",
    )(p, p, p, conv_w, conv_b.reshape(1, d))


def _merge_kernel(a_ref, b_ref, wa_ref, wb_ref, ga_ref, gb_ref, o_ref):
    y_a = _dot(a_ref[...], wa_ref[...])
    y_b = _dot(b_ref[...], wb_ref[...])
    o_ref[...] = (jax.nn.sigmoid(ga_ref[...]) * y_a + jax.nn.sigmoid(gb_ref[...]) * y_b).astype(o_ref.dtype)


def _merge(a_in, b_in, w_a, w_b, p, off_ga, off_gb, tm=1024, tn=512):
    n, d = a_in.shape
    tm = min(tm, n)
    return pl.pallas_call(
        _merge_kernel,
        grid=(n // tm, d // tn),
        in_specs=[pl.BlockSpec((tm, d), lambda i, j: (i, 0)),
                  pl.BlockSpec((tm, d), lambda i, j: (i, 0)),
                  pl.BlockSpec((d, tn), lambda i, j: (0, j)),
                  pl.BlockSpec((d, tn), lambda i, j: (0, j)),
                  pl.BlockSpec((tm, tn), lambda i, j: (i, off_ga // tn + j)),
                  pl.BlockSpec((tm, tn), lambda i, j: (i, off_gb // tn + j))],
        out_specs=pl.BlockSpec((tm, tn), lambda i, j: (i, j)),
        out_shape=jax.ShapeDtypeStruct((n, d), BF16),
        compiler_params=_params("parallel", "parallel"),
        name="merge",
    )(a_in, b_in, w_a, w_b, p, p)


def _residual(y, x, post_g, mod_ref, gate_row):
    return x + mod_ref[0, gate_row:gate_row + 1, :] * (_rms(y) * post_g)


def _outproj_kernel(m_ref, w_ref, x_ref, postg_ref, preg_ref, mod_ref, xo_ref, h_ref):
    y = _dot(m_ref[0], w_ref[...])
    x_new = _residual(y, x_ref[0], postg_ref[...], mod_ref, G1)
    xo_ref[0] = x_new
    h_ref[0] = _modulated_norm(x_new, preg_ref[...], mod_ref, SH2, SC2).astype(h_ref.dtype)


def _outproj(merged, w_o, x, post_g, pre_g, mod, h_dtype, tm=512):
    bsz, t, d = x.shape
    tm = min(tm, t)
    return pl.pallas_call(
        _outproj_kernel,
        grid=(bsz, t // tm),
        in_specs=[pl.BlockSpec((1, tm, d), lambda b, i: (b, i, 0)),
                  pl.BlockSpec((d, d), lambda b, i: (0, 0)),
                  pl.BlockSpec((1, tm, d), lambda b, i: (b, i, 0)),
                  pl.BlockSpec((1, d), lambda b, i: (0, 0)),
                  pl.BlockSpec((1, d), lambda b, i: (0, 0)),
                  pl.BlockSpec((1, 6, d), lambda b, i: (b, 0, 0))],
        out_specs=(pl.BlockSpec((1, tm, d), lambda b, i: (b, i, 0)),
                   pl.BlockSpec((1, tm, d), lambda b, i: (b, i, 0))),
        out_shape=(jax.ShapeDtypeStruct((bsz, t, d), F32),
                   jax.ShapeDtypeStruct((bsz, t, d), h_dtype)),
        compiler_params=_params("parallel", "parallel"),
        name="out_proj",
    )(merged.reshape(bsz, t, d), w_o, x, post_g.reshape(1, d), pre_g.reshape(1, d), mod)


def _ffn_res_kernel(*refs, with_next):
    if with_next:
        f_ref, x_ref, postg_ref, mod_ref, preg_ref, modn_ref, xo_ref, h_ref = refs
    else:
        f_ref, x_ref, postg_ref, mod_ref, xo_ref = refs
    x_new = _residual(f_ref[0], x_ref[0], postg_ref[...], mod_ref, G2)
    xo_ref[0] = x_new
    if with_next:
        h_ref[0] = _modulated_norm(x_new, preg_ref[...], modn_ref, SH1, SC1).astype(h_ref.dtype)


def _ffn_residual(f, x, post_g, mod, next_pre_g=None, next_mod=None, tt=256):
    bsz, t, d = x.shape
    with_next = next_pre_g is not None
    tile = pl.BlockSpec((1, tt, d), lambda b, i: (b, i, 0))
    vec = pl.BlockSpec((1, d), lambda b, i: (0, 0))
    modspec = pl.BlockSpec((1, 6, d), lambda b, i: (b, 0, 0))
    in_specs = [tile, tile, vec, modspec]
    args = [f.reshape(bsz, t, d), x, post_g.reshape(1, d), mod]
    out_shape = [jax.ShapeDtypeStruct((bsz, t, d), F32)]
    out_specs = [tile]
    if with_next:
        in_specs += [vec, modspec]
        args += [next_pre_g.reshape(1, d), next_mod]
        out_shape.append(jax.ShapeDtypeStruct((bsz, t, d), BF16))
        out_specs.append(tile)
    return pl.pallas_call(
        functools.partial(_ffn_res_kernel, with_next=with_next),
        grid=(bsz, t // tt),
        in_specs=in_specs,
        out_specs=tuple(out_specs),
        out_shape=tuple(out_shape),
        compiler_params=_params("parallel", "parallel"),
        name="ffn_residual",
    )(*args)


def _ffn_kernel(h_ref, w1_ref, w3_ref, w2_ref, o_ref):
    j = pl.program_id(1)
    h = h_ref[...]
    a = _dot(h, w1_ref[...].astype(BF16))
    b = _dot(h, w3_ref[...].astype(BF16))
    y = _dot((_silu(a) * b).astype(BF16), w2_ref[...].astype(BF16))

    @pl.when(j == 0)
    def _():
        o_ref[...] = y

    @pl.when(j > 0)
    def _():
        o_ref[...] += y


def _ffn(h, w1, w3, w2, layer_idx, tm=1024, tf=256):
    n, d = h.shape
    f = w1.shape[-1]
    tm = min(tm, n)
    return pl.pallas_call(
        _ffn_kernel,
        grid=(n // tm, f // tf),
        in_specs=[pl.BlockSpec((tm, d), lambda i, j: (i, 0)),
                  pl.BlockSpec((None, d, tf), lambda i, j: (layer_idx, 0, j)),
                  pl.BlockSpec((None, d, tf), lambda i, j: (layer_idx, 0, j)),
                  pl.BlockSpec((None, tf, d), lambda i, j: (layer_idx, j, 0))],
        out_specs=pl.BlockSpec((tm, d), lambda i, j: (i, 0)),
        out_shape=jax.ShapeDtypeStruct((n, d), F32),
        compiler_params=_params("parallel", "arbitrary"),
        name="ffn",
    )(h, w1, w3, w2)


def _router_kernel(h_ref, w_ref, o_ref, *, n_experts):
    h = h_ref[...]
    w = w_ref[...]
    h_hi = h.astype(BF16)
    h_lo = (h - h_hi.astype(F32)).astype(BF16)
    w_hi = w.astype(BF16)
    w_lo = (w - w_hi.astype(F32)).astype(BF16)
    logits = _dot(h_hi, w_hi) + (_dot(h_hi, w_lo) + _dot(h_lo, w_hi))
    lane = lax.broadcasted_iota(jnp.int32, logits.shape, 1)
    neg = -jnp.inf
    l1 = jnp.where(lane < n_experts, logits, neg)
    m1 = jnp.max(l1, axis=-1, keepdims=True)
    i1 = jnp.min(jnp.where(l1 == m1, lane, LANES), axis=-1, keepdims=True)
    l2 = jnp.where(lane == i1, neg, l1)
    m2 = jnp.max(l2, axis=-1, keepdims=True)
    i2 = jnp.min(jnp.where(l2 == m2, lane, LANES), axis=-1, keepdims=True)
    e = jnp.exp(m2 - m1)
    p1 = 1.0 / (1.0 + e)
    p2 = e / (1.0 + e)
    out = jnp.where(lane == 0, i1.astype(F32),
                    jnp.where(lane == 1, i2.astype(F32),
                              jnp.where(lane == 2, p1, jnp.where(lane == 3, p2, 0.0))))
    o_ref[...] = out


def _router(h, w_router, tm=512):
    n, d = h.shape
    n_experts = w_router.shape[1]
    w_pad = jnp.zeros((d, LANES), F32).at[:, :n_experts].set(w_router)
    return pl.pallas_call(
        functools.partial(_router_kernel, n_experts=n_experts),
        grid=(n // tm,),
        in_specs=[pl.BlockSpec((tm, d), lambda i: (i, 0)),
                  pl.BlockSpec((d, LANES), lambda i: (0, 0))],
        out_specs=pl.BlockSpec((tm, LANES), lambda i: (i, 0)),
        out_shape=jax.ShapeDtypeStruct((n, LANES), F32),
        compiler_params=_params("parallel"),
        name="router",
    )(h, w_pad)


def _dispatch_kernel(pos_ref, h_hbm, xg_in_hbm, xg_hbm, sem, *, tt):
    del xg_in_hbm
    base = pl.program_id(0) * tt

    def row_copy(src_row, dst_row):
        return pltpu.make_async_copy(h_hbm.at[pl.ds(src_row, 1)], xg_hbm.at[pl.ds(dst_row, 1)], sem)

    def start(r, carry):
        for k in range(TOP_K):
            row_copy(base + r, pos_ref[0, 0, TOP_K * r + k]).start()
        return carry

    def wait(r, carry):
        for k in range(TOP_K):
            row_copy(base + r, pos_ref[0, 0, TOP_K * r + k]).wait()
        return carry

    lax.fori_loop(0, tt, start, 0)
    lax.fori_loop(0, tt, wait, 0)


def _dispatch(h, pos, n_rows, tt=256):
    n, d = h.shape
    pos3 = pos.reshape(n // tt, 1, TOP_K * tt)
    return pl.pallas_call(
        functools.partial(_dispatch_kernel, tt=tt),
        grid=(n // tt,),
        in_specs=[pl.BlockSpec((1, 1, TOP_K * tt), lambda i: (i, 0, 0), memory_space=pltpu.SMEM),
                  pl.BlockSpec(memory_space=pl.ANY),
                  pl.BlockSpec(memory_space=pl.ANY)],
        out_specs=pl.BlockSpec(memory_space=pl.ANY),
        out_shape=jax.ShapeDtypeStruct((n_rows, d), F32),
        scratch_shapes=[pltpu.SemaphoreType.DMA(())],
        input_output_aliases={2: 0},
        compiler_params=_params("arbitrary"),
        name="moe_dispatch",
    )(pos3, h, jnp.zeros((n_rows, d), F32))


def _moe_gate_kernel(exp_ref, first_ref, valid_ref, x_ref, w1_ref, w3_ref, g_ref, w1b, w3b):
    i = pl.program_id(1)

    @pl.when(first_ref[i] == 1)
    def _():
        w1b[...] = w1_ref[...].astype(BF16)
        w3b[...] = w3_ref[...].astype(BF16)

    @pl.when(valid_ref[i] == 1)
    def _():
        x = x_ref[...].astype(BF16)
        a = _dot(x, w1b[...])
        b = _dot(x, w3b[...])
        g_ref[...] = (_silu(a) * b).astype(g_ref.dtype)

    @pl.when(valid_ref[i] == 0)
    def _():
        g_ref[...] = jnp.zeros_like(g_ref)


def _moe_down_kernel(exp_ref, first_ref, valid_ref, g_ref, w2_ref, y_ref, w2b):
    i = pl.program_id(1)

    @pl.when(first_ref[i] == 1)
    def _():
        w2b[...] = w2_ref[...].astype(BF16)

    @pl.when(valid_ref[i] == 1)
    def _():
        y_ref[...] = _dot(g_ref[...], w2b[...])

    @pl.when(valid_ref[i] == 0)
    def _():
        y_ref[...] = jnp.zeros_like(y_ref)


def _moe_experts(xg, tables, w1, w3, w2, moe_idx, sub, tf=512, tn=512):
    n_rows, d = xg.shape
    f = w1.shape[-1]
    n_items = n_rows // sub
    g = pl.pallas_call(
        _moe_gate_kernel,
        grid_spec=pltpu.PrefetchScalarGridSpec(
            num_scalar_prefetch=3,
            grid=(f // tf, n_items),
            in_specs=[pl.BlockSpec((sub, d), lambda j, i, exp, first, valid: (i, 0)),
                      pl.BlockSpec((None, None, d, tf), lambda j, i, exp, first, valid: (moe_idx, exp[i], 0, j)),
                      pl.BlockSpec((None, None, d, tf), lambda j, i, exp, first, valid: (moe_idx, exp[i], 0, j))],
            out_specs=pl.BlockSpec((sub, tf), lambda j, i, exp, first, valid: (i, j)),
            scratch_shapes=[pltpu.VMEM((d, tf), BF16), pltpu.VMEM((d, tf), BF16)]),
        out_shape=jax.ShapeDtypeStruct((n_rows, f), BF16),
        compiler_params=_params("arbitrary", "arbitrary"),
        name="moe_gate",
    )(*tables, xg, w1, w3)
    return pl.pallas_call(
        _moe_down_kernel,
        grid_spec=pltpu.PrefetchScalarGridSpec(
            num_scalar_prefetch=3,
            grid=(d // tn, n_items),
            in_specs=[pl.BlockSpec((sub, f), lambda j, i, exp, first, valid: (i, 0)),
                      pl.BlockSpec((None, None, f, tn), lambda j, i, exp, first, valid: (moe_idx, exp[i], 0, j))],
            out_specs=pl.BlockSpec((sub, tn), lambda j, i, exp, first, valid: (i, j)),
            scratch_shapes=[pltpu.VMEM((f, tn), BF16)]),
        out_shape=jax.ShapeDtypeStruct((n_rows, d), F32),
        compiler_params=_params("arbitrary", "arbitrary"),
        name="moe_down",
    )(*tables, g, w2)


def _combine_kernel(pos_ref, y_hbm, route_ref, x_ref, postg_ref, mod_ref, xo_ref, buf, sem, *, tt, tiles_per_batch):
    def row_copy(k, r):
        return pltpu.make_async_copy(y_hbm.at[pl.ds(pos_ref[0, 0, TOP_K * r + k], 1)],
                                     buf.at[k, pl.ds(r, 1)], sem)

    def start(r, carry):
        for k in range(TOP_K):
            row_copy(k, r).start()
        return carry

    def wait(r, carry):
        for k in range(TOP_K):
            row_copy(k, r).wait()
        return carry

    lax.fori_loop(0, tt, start, 0)
    lax.fori_loop(0, tt, wait, 0)
    route = route_ref[...]
    f = route[:, 2:3] * buf[0] + route[:, 3:4] * buf[1]
    xo_ref[0] = _residual(f, x_ref[0], postg_ref[...], mod_ref, G2)


def _combine(y, pos, route, x, post_g, mod, tt=256):
    bsz, t, d = x.shape
    n = bsz * t
    tpb = t // tt
    pos3 = pos.reshape(n // tt, 1, TOP_K * tt)
    return pl.pallas_call(
        functools.partial(_combine_kernel, tt=tt, tiles_per_batch=tpb),
        grid=(bsz, tpb),
        in_specs=[pl.BlockSpec((1, 1, TOP_K * tt), lambda b, i: (b * tpb + i, 0, 0), memory_space=pltpu.SMEM),
                  pl.BlockSpec(memory_space=pl.ANY),
                  pl.BlockSpec((tt, LANES), lambda b, i: (b * tpb + i, 0)),
                  pl.BlockSpec((1, tt, d), lambda b, i: (b, i, 0)),
                  pl.BlockSpec((1, d), lambda b, i: (0, 0)),
                  pl.BlockSpec((1, 6, d), lambda b, i: (b, 0, 0))],
        out_specs=pl.BlockSpec((1, tt, d), lambda b, i: (b, i, 0)),
        out_shape=jax.ShapeDtypeStruct((bsz, t, d), F32),
        scratch_shapes=[pltpu.VMEM((TOP_K, tt, d), F32), pltpu.SemaphoreType.DMA(())],
        compiler_params=_params("arbitrary", "arbitrary"),
        name="moe_combine",
    )(pos3, y, route, x, post_g.reshape(1, d), mod)


def _routing_tables(route, n_experts, sub, n_items):
    experts = route[:, :TOP_K].astype(jnp.int32).reshape(-1)
    onehot = (experts[:, None] == jnp.arange(n_experts, dtype=jnp.int32)[None, :]).astype(jnp.int32)
    csum = jnp.cumsum(onehot, axis=0)
    rank = jnp.sum(csum * onehot, axis=1) - 1
    counts = csum[-1]
    blocks = (counts + sub - 1) // sub
    block_end = jnp.cumsum(blocks)
    offs = (block_end - blocks) * sub
    pos = offs[experts] + rank
    n_valid = block_end[-1]
    item = jnp.arange(n_items, dtype=jnp.int32)
    blk = jnp.minimum(item, n_valid - 1)
    exp = jnp.minimum(jnp.searchsorted(block_end, blk, side="right"), n_experts - 1).astype(jnp.int32)
    valid = (item < n_valid).astype(jnp.int32)
    prev = jnp.concatenate([jnp.full((1,), -1, jnp.int32), exp[:-1]])
    first = ((exp != prev) & (valid == 1)).astype(jnp.int32)
    return pos.astype(jnp.int32), (exp, first, valid)


def _moe(h, x, router_w, w1, w3, w2, moe_idx, post_g, mod, sub=512):
    bsz, t, d = x.shape
    n = bsz * t
    n_experts = router_w.shape[-1]
    n_items = (n * TOP_K) // sub + n_experts
    hf = h.reshape(n, d)
    route = _router(hf, router_w[moe_idx])
    pos, tables = _routing_tables(route, n_experts, sub, n_items)
    xg = _dispatch(hf, pos, n_items * sub)
    y = _moe_experts(xg, tables, w1, w3, w2, moe_idx, sub)
    return _combine(y, pos, route, x, post_g, mod)


def kernel(x, c, ctx, c_ctx, w_ada, b_ada, pre_mix_g, post_mix_g, pre_ffn_g, post_ffn_g, w_in, gate_w2_f, gate_b_f, gate_w2_b, gate_b_b, gla_norm_g, conv_w, conv_b, w_proj_a, w_proj_b, w_out, ffn_w1, ffn_w3, ffn_w2, router_w, moe_w1, moe_w3, moe_w2):
    bsz, seq, d = x.shape
    ctx_len = ctx.shape[1]
    depth = w_ada.shape[0]
    dk = gate_w2_f.shape[-1]
    dv = gla_norm_g.shape[-1]
    hk, hv = dk // N_HEADS, dv // N_HEADS
    off_lr = 2 * dk + dv
    n_lr = 2 * GATE_RANK
    ctx_cols = 2 * dk + dv
    off_r = ctx_cols
    off_cb = off_r + dv
    off_ga = off_cb + 3 * d
    off_gb = off_ga + d
    n_main = off_gb + d

    cc = jnp.concatenate([c, c_ctx[None, :], jnp.zeros((8 - bsz - 1, d), F32)], axis=0)
    mod_all = _ada(cc, w_ada, b_ada)

    xc = ctx
    hx = hc = None
    for layer in range(depth):
        last = layer == depth - 1
        mod_x = mod_all[layer, :bsz].reshape(bsz, 6, d)
        mod_c = jnp.broadcast_to(mod_all[layer, bsz].reshape(1, 6, d), (bsz, 6, d))

        w_in_l = w_in[layer]
        w_main = jnp.concatenate([w_in_l[:, :off_lr], w_in_l[:, off_lr + n_lr:]], axis=1).astype(BF16)
        w_lr = jnp.zeros((d, LANES), BF16).at[:, :n_lr].set(w_in_l[:, off_lr:off_lr + n_lr].astype(BF16))
        w2 = jnp.zeros((LANES, 2 * dk), BF16)
        w2 = w2.at[:GATE_RANK, :dk].set(gate_w2_f[layer].astype(BF16))
        w2 = w2.at[GATE_RANK:n_lr, dk:].set(gate_w2_b[layer].astype(BF16))
        gate_b = jnp.concatenate([gate_b_f[layer], gate_b_b[layer]]).reshape(1, 2 * dk)
        w_a = w_proj_a[layer].astype(BF16)
        w_b = w_proj_b[layer].astype(BF16)
        w_o = w_out[layer].astype(BF16)

        if hx is None:
            hx = _prenorm(x, pre_mix_g[layer], mod_x)
            hc = _prenorm(xc, pre_mix_g[layer], mod_c)

        hc2 = hc.reshape(bsz * ctx_len, d)
        pc = _matmul(hc2, w_main, ctx_cols if last else n_main).reshape(bsz, ctx_len, -1)
        la_c = _decay(hc2, w_lr, w2, gate_b).reshape(bsz, ctx_len, 2 * dk)
        s0 = jnp.zeros((bsz, N_HEADS, hv, hk), F32)
        if last:
            s_f, s_b = _gla(pc, la_c, s0, s0, None, dk, dv, off_r, want_out=False)
        else:
            o_c, s_f, s_b = _gla(pc, la_c, s0, s0, gla_norm_g[layer], dk, dv, off_r, want_out=True)

        hx2 = hx.reshape(bsz * seq, d)
        px = _matmul(hx2, w_main, n_main).reshape(bsz, seq, n_main)
        la_x = _decay(hx2, w_lr, w2, gate_b).reshape(bsz, seq, 2 * dk)
        o_x, _, _ = _gla(px, la_x, s_f, s_b, gla_norm_g[layer], dk, dv, off_r, want_out=True)
        a_x = _gated_conv(px, conv_w[layer], conv_b[layer], d, off_cb, seg=GRID_W)
        merged = _merge(a_x.reshape(bsz * seq, d), o_x.reshape(bsz * seq, d), w_a, w_b,
                        px.reshape(bsz * seq, n_main), off_ga, off_gb)
        moe_layer = layer % 2 == 1
        x, h2 = _outproj(merged, w_o, x, post_mix_g[layer], pre_ffn_g[layer], mod_x,
                         F32 if moe_layer else BF16)
        nxt = None if last else (pre_mix_g[layer + 1], mod_all[layer + 1, :bsz].reshape(bsz, 6, d))
        if moe_layer:
            x_new = _moe(h2, x, router_w, moe_w1, moe_w3, moe_w2, layer // 2, post_ffn_g[layer], mod_x)
            if nxt is None:
                x = x_new
            else:
                x = x_new
                hx = _prenorm(x, nxt[0], nxt[1])
        else:
            f_x = _ffn(h2.reshape(bsz * seq, d), ffn_w1, ffn_w3, ffn_w2, layer // 2)
            if nxt is None:
                (x,) = _ffn_residual(f_x, x, post_ffn_g[layer], mod_x)
            else:
                x, hx = _ffn_residual(f_x, x, post_ffn_g[layer], mod_x, nxt[0], nxt[1])

        if not last:
            a_c = _gated_conv(pc, conv_w[layer], conv_b[layer], d, off_cb, seg=ctx_len, tt=ctx_len)
            merged_c = _merge(a_c.reshape(bsz * ctx_len, d), o_c.reshape(bsz * ctx_len, d), w_a, w_b,
                              pc.reshape(bsz * ctx_len, n_main), off_ga, off_gb)
            xc, h2c = _outproj(merged_c, w_o, xc, post_mix_g[layer], pre_ffn_g[layer], mod_c,
                               F32 if moe_layer else BF16)
            mod_cn = jnp.broadcast_to(mod_all[layer + 1, bsz].reshape(1, 6, d), (bsz, 6, d))
            if moe_layer:
                xc = _moe(h2c, xc, router_w, moe_w1, moe_w3, moe_w2, layer // 2, post_ffn_g[layer], mod_c)
                hc = _prenorm(xc, pre_mix_g[layer + 1], mod_cn)
            else:
                f_c = _ffn(h2c.reshape(bsz * ctx_len, d), ffn_w1, ffn_w3, ffn_w2, layer // 2)
                xc, hc = _ffn_residual(f_c, xc, post_ffn_g[layer], mod_c, pre_mix_g[layer + 1], mod_cn)
    return x
```

```python
import functools

import jax
import jax.numpy as jnp
from jax import lax
from jax.experimental import pallas as pl
from jax.experimental.pallas import tpu as pltpu

F32 = jnp.float32
BF16 = jnp.bfloat16

EPS = 1e-6
N_HEADS = 4
GATE_RANK = 16
GATE_TAU = 16.0
GLA_CHUNK = 64
GRID_W = 64
TOP_K = 2

LANES = 128
VMEM_LIMIT = 56 * 1024 * 1024

SH1, SC1, G1, SH2, SC2, G2 = range(6)


def _params(*sem):
    return pltpu.CompilerParams(dimension_semantics=sem, vmem_limit_bytes=VMEM_LIMIT)


def _dot(a, b):
    return jnp.dot(a, b, preferred_element_type=F32)


def _dot_nt(a, b):
    return lax.dot_general(a, b, (((1,), (1,)), ((), ())), preferred_element_type=F32)


def _dot_tn(a, b):
    return lax.dot_general(a, b, (((0,), (0,)), ((), ())), preferred_element_type=F32)


def _silu(x):
    return x * jax.nn.sigmoid(x)


def _rms(x):
    return x * lax.rsqrt(jnp.mean(x * x, axis=-1, keepdims=True) + EPS)


def _ada_kernel(c_ref, w_ref, b_ref, o_ref):
    s = _silu(c_ref[...]).astype(BF16)
    o_ref[0] = _dot(s, w_ref[0].astype(BF16)) + b_ref[0]


def _ada(cc, w_ada, b_ada, tn=1024):
    depth, d, n = w_ada.shape
    rows = cc.shape[0]
    return pl.pallas_call(
        _ada_kernel,
        grid=(depth, n // tn),
        in_specs=[pl.BlockSpec((rows, d), lambda l, j: (0, 0)),
                  pl.BlockSpec((1, d, tn), lambda l, j: (l, 0, j)),
                  pl.BlockSpec((1, 1, tn), lambda l, j: (l, 0, j))],
        out_specs=pl.BlockSpec((1, rows, tn), lambda l, j: (l, 0, j)),
        out_shape=jax.ShapeDtypeStruct((depth, rows, n), F32),
        compiler_params=_params("parallel", "parallel"),
        name="ada",
    )(cc, w_ada, b_ada.reshape(depth, 1, n))


def _modulated_norm(x, g, mod_ref, shift_row, scale_row):
    h = _rms(x) * g
    return h * (1.0 + mod_ref[0, scale_row:scale_row + 1, :]) + mod_ref[0, shift_row:shift_row + 1, :]


def _prenorm_kernel(x_ref, g_ref, mod_ref, o_ref):
    o_ref[0] = _modulated_norm(x_ref[0], g_ref[...], mod_ref, SH1, SC1).astype(o_ref.dtype)


def _prenorm(x, g, mod, tt=256):
    b, t, d = x.shape
    return pl.pallas_call(
        _prenorm_kernel,
        grid=(b, t // tt),
        in_specs=[pl.BlockSpec((1, tt, d), lambda i, j: (i, j, 0)),
                  pl.BlockSpec((1, d), lambda i, j: (0, 0)),
                  pl.BlockSpec((1, 6, d), lambda i, j: (i, 0, 0))],
        out_specs=pl.BlockSpec((1, tt, d), lambda i, j: (i, j, 0)),
        out_shape=jax.ShapeDtypeStruct((b, t, d), BF16),
        compiler_params=_params("parallel", "parallel"),
        name="prenorm",
    )(x, g.reshape(1, d), mod)


def _mm_kernel(h_ref, w_ref, o_ref):
    o_ref[...] = _dot(h_ref[...], w_ref[...])


def _matmul(h, w, ncols, tm=1024, tn=1024):
    n, k = h.shape
    tm = min(tm, n)
    return pl.pallas_call(
        _mm_kernel,
        grid=(n // tm, ncols // tn),
        in_specs=[pl.BlockSpec((tm, k), lambda i, j: (i, 0)),
                  pl.BlockSpec((k, tn), lambda i, j: (0, j))],
        out_specs=pl.BlockSpec((tm, tn), lambda i, j: (i, j)),
        out_shape=jax.ShapeDtypeStruct((n, ncols), F32),
        compiler_params=_params("parallel", "parallel"),
        name="in_proj",
    )(h, w)


def _decay_kernel(h_ref, wlr_ref, w2_ref, b_ref, o_ref):
    lr = _dot(h_ref[...], wlr_ref[...]).astype(BF16)
    z = _dot(lr, w2_ref[...]) + b_ref[...]
    log_sig = jnp.minimum(z, 0.0) - jnp.log1p(jnp.exp(-jnp.abs(z)))
    o_ref[...] = log_sig / GATE_TAU


def _decay(h, w_lr, w2, bias, tm=512):
    n, k = h.shape
    r = w_lr.shape[1]
    m = w2.shape[1]
    return pl.pallas_call(
        _decay_kernel,
        grid=(n // tm,),
        in_specs=[pl.BlockSpec((tm, k), lambda i: (i, 0)),
                  pl.BlockSpec((k, r), lambda i: (0, 0)),
                  pl.BlockSpec((r, m), lambda i: (0, 0)),
                  pl.BlockSpec((1, m), lambda i: (0, 0))],
        out_specs=pl.BlockSpec((tm, m), lambda i: (i, 0)),
        out_shape=jax.ShapeDtypeStruct((n, m), F32),
        compiler_params=_params("parallel"),
        name="decay",
    )(h, w_lr, w2, bias)


def _gla_kernel(*refs, t, scale, want_out):
    if want_out:
        (q_ref, k_ref, v_ref, laf_ref, lab_ref, s0f_ref, s0b_ref, r_ref, g_ref,
         o_ref, sf_ref, sb_ref,
         of_scr, ob_scr, qtf_scr, qtb_scr, kef_scr, keb_scr, df_scr, db_scr, stf_scr, stb_scr) = refs
    else:
        (q_ref, k_ref, v_ref, laf_ref, lab_ref, s0f_ref, s0b_ref,
         sf_ref, sb_ref, kef_scr, keb_scr, df_scr, db_scr, stf_scr, stb_scr) = refs
        of_scr = ob_scr = qtf_scr = qtb_scr = None
    c_len = GLA_CHUNK
    n_chunks = t // c_len
    blk_len = min(4 * c_len, t)
    n_blocks = t // blk_len
    per_blk = blk_len // c_len
    hk = k_ref.shape[-1]
    row = lax.broadcasted_iota(jnp.int32, (blk_len, blk_len), 0)
    col = lax.broadcasted_iota(jnp.int32, (blk_len, blk_len), 1)
    same_chunk = (row // c_len) == (col // c_len)
    mask_f = same_chunk & (row >= col)
    mask_b = same_chunk & (row <= col)
    tri_f = mask_f.astype(BF16)
    tri_b = mask_b.astype(BF16)

    def prepare(blk, la_ref, tri, mask, last_row, qt_scr, ke_scr, d_scr, o_scr):
        sl = pl.ds(pl.multiple_of(blk * blk_len, blk_len), blk_len)
        la = la_ref[0, sl, :]
        la_hi = la.astype(BF16)
        la_lo = (la - la_hi.astype(F32)).astype(BF16)
        b = _dot(tri, la_hi) + _dot(tri, la_lo)
        b_last = jnp.concatenate(
            [jnp.broadcast_to(b[i * c_len + last_row:i * c_len + last_row + 1, :], (c_len, hk))
             for i in range(per_blk)], axis=0)
        k = k_ref[0, sl, :]
        ke_scr[sl, :] = (k * jnp.exp(b_last - b)).astype(BF16)
        d = jnp.exp(b_last)
        for i in range(per_blk):
            d_scr[pl.ds(pl.multiple_of((blk * per_blk + i) * 8, 8), 8), :] = d[i * c_len:i * c_len + 8, :]
        if want_out:
            qt = (q_ref[0, sl, :] * scale * jnp.exp(b)).astype(BF16)
            qt_scr[sl, :] = qt
            kt = (k * jnp.exp(-b)).astype(BF16)
            att = jnp.where(mask, _dot_nt(qt, kt), 0.0).astype(BF16)
            o_scr[sl, :] = _dot(att, v_ref[0, sl, :].astype(BF16))

    def prepare_body(blk, carry):
        prepare(blk, laf_ref, tri_f, mask_f, c_len - 1, qtf_scr, kef_scr, df_scr, of_scr)
        prepare(blk, lab_ref, tri_b, mask_b, 0, qtb_scr, keb_scr, db_scr, ob_scr)
        return carry

    lax.fori_loop(0, n_blocks, prepare_body, 0)

    stf_scr[...] = s0f_ref[0, 0]
    stb_scr[...] = s0b_ref[0, 0]

    def scan_step(c, qt_scr, ke_scr, d_scr, o_scr, st_scr):
        sl = pl.ds(pl.multiple_of(c * c_len, c_len), c_len)
        st = st_scr[...]
        if want_out:
            o_scr[sl, :] += _dot_nt(qt_scr[sl, :], st.astype(BF16))
        d = d_scr[pl.ds(pl.multiple_of(c * 8, 8), 8), :]
        st_scr[...] = d[0:1, :] * st + _dot_tn(v_ref[0, sl, :].astype(BF16), ke_scr[sl, :])

    def scan_body(c, carry):
        scan_step(c, qtf_scr, kef_scr, df_scr, of_scr, stf_scr)
        scan_step(n_chunks - 1 - c, qtb_scr, keb_scr, db_scr, ob_scr, stb_scr)
        return carry

    lax.fori_loop(0, n_chunks, scan_body, 0, unroll=2)
    sf_ref[0, 0] = stf_scr[...]
    sb_ref[0, 0] = stb_scr[...]

    if want_out:
        def epilogue(c, carry):
            sl = pl.ds(pl.multiple_of(c * c_len, c_len), c_len)
            o = _rms(of_scr[sl, :] + ob_scr[sl, :]) * g_ref[...]
            o_ref[0, sl, :] = (o * _silu(r_ref[0, sl, :])).astype(o_ref.dtype)
            return carry

        lax.fori_loop(0, n_chunks, epilogue, 0)


def _gla(p, la, s0f, s0b, gla_g, dk_total, dv_total, off_r, want_out):
    bsz, t, _ = p.shape
    hk = dk_total // N_HEADS
    hv = dv_total // N_HEADS
    nkb = dk_total // hk
    kern = functools.partial(_gla_kernel, t=t, scale=hk ** -0.5, want_out=want_out)
    n_chunks = t // GLA_CHUNK
    in_specs = [
        pl.BlockSpec((1, t, hk), lambda b, h: (b, 0, h)),
        pl.BlockSpec((1, t, hk), lambda b, h: (b, 0, nkb + h)),
        pl.BlockSpec((1, t, hv), lambda b, h: (b, 0, 2 * dk_total // hv + h)),
        pl.BlockSpec((1, t, hk), lambda b, h: (b, 0, h)),
        pl.BlockSpec((1, t, hk), lambda b, h: (b, 0, nkb + h)),
        pl.BlockSpec((1, 1, hv, hk), lambda b, h: (b, h, 0, 0)),
        pl.BlockSpec((1, 1, hv, hk), lambda b, h: (b, h, 0, 0)),
    ]
    args = [p, p, p, la, la, s0f, s0b]
    st_shape = jax.ShapeDtypeStruct((bsz, N_HEADS, hv, hk), F32)
    st_spec = pl.BlockSpec((1, 1, hv, hk), lambda b, h: (b, h, 0, 0))
    scratch = [pltpu.VMEM((t, hk), BF16), pltpu.VMEM((t, hk), BF16),
               pltpu.VMEM((n_chunks * 8, hk), F32), pltpu.VMEM((n_chunks * 8, hk), F32),
               pltpu.VMEM((hv, hk), F32), pltpu.VMEM((hv, hk), F32)]
    if want_out:
        in_specs += [pl.BlockSpec((1, t, hv), lambda b, h: (b, 0, off_r // hv + h)),
                     pl.BlockSpec((1, hv), lambda b, h: (0, h))]
        args += [p, gla_g.reshape(1, dv_total)]
        out_shape = (jax.ShapeDtypeStruct((bsz, t, dv_total), BF16), st_shape, st_shape)
        out_specs = (pl.BlockSpec((1, t, hv), lambda b, h: (b, 0, h)), st_spec, st_spec)
        scratch = [pltpu.VMEM((t, hv), F32), pltpu.VMEM((t, hv), F32),
                   pltpu.VMEM((t, hk), BF16), pltpu.VMEM((t, hk), BF16)] + scratch
    else:
        out_shape = (st_shape, st_shape)
        out_specs = (st_spec, st_spec)
    return pl.pallas_call(
        kern,
        grid=(bsz, N_HEADS),
        in_specs=in_specs,
        out_specs=out_specs,
        out_shape=out_shape,
        scratch_shapes=scratch,
        compiler_params=_params("parallel", "parallel"),
        name="gla",
    )(*args)


def _conv_kernel(cb_ref, cc_ref, cx_ref, w_ref, b_ref, o_ref, *, seg):
    u = cc_ref[0] * cx_ref[0]
    tt = u.shape[0]
    pos = lax.broadcasted_iota(jnp.int32, u.shape, 0) % seg
    u_prev = jnp.where(pos == 0, 0.0, pltpu.roll(u, 1, axis=0))
    u_next = jnp.where(pos == seg - 1, 0.0, pltpu.roll(u, tt - 1, axis=0))
    conv = b_ref[...] + u_prev * w_ref[0:1, :]
    conv = conv + u * w_ref[1:2, :]
    conv = conv + u_next * w_ref[2:3, :]
    o_ref[0] = (cb_ref[0] * conv).astype(o_ref.dtype)


def _gated_conv(p, conv_w, conv_b, d, off_cb, seg, tt=256):
    bsz, t, _ = p.shape
    cb = off_cb // d
    kern = functools.partial(_conv_kernel, seg=seg)
    return pl.pallas_call(
        kern,
        grid=(bsz, t // tt),
        in_specs=[pl.BlockSpec((1, tt, d), lambda b, i: (b, i, cb)),
                  pl.BlockSpec((1, tt, d), lambda b, i: (b, i, cb + 1)),
                  pl.BlockSpec((1, tt, d), lambda b, i: (b, i, cb + 2)),
                  pl.BlockSpec((3, d), lambda b, i: (0, 0)),
                  pl.BlockSpec((1, d), lambda b, i: (0, 0))],
        out_specs=pl.BlockSpec((1, tt, d), lambda b, i: (b, i, 0)),
        out_shape=jax.ShapeDtypeStruct((bsz, t, d), BF16),
        compiler_params=_params("parallel", "parallel"),
        name="gated_conv",
    )(p, p, p, conv_w, conv_b.reshape(1, d))


def _merge_kernel(a_ref, b_ref, wa_ref, wb_ref, ga_ref, gb_ref, o_ref):
    y_a = _dot(a_ref[...], wa_ref[...])
    y_b = _dot(b_ref[...], wb_ref[...])
    o_ref[...] = (jax.nn.sigmoid(ga_ref[...]) * y_a + jax.nn.sigmoid(gb_ref[...]) * y_b).astype(o_ref.dtype)


def _merge(a_in, b_in, w_a, w_b, p, off_ga, off_gb, tm=1024, tn=512):
    n, d = a_in.shape
    tm = min(tm, n)
    return pl.pallas_call(
        _merge_kernel,
        grid=(n // tm, d // tn),
        in_specs=[pl.BlockSpec((tm, d), lambda i, j: (i, 0)),
                  pl.BlockSpec((tm, d), lambda i, j: (i, 0)),
                  pl.BlockSpec((d, tn), lambda i, j: (0, j)),
                  pl.BlockSpec((d, tn), lambda i, j: (0, j)),
                  pl.BlockSpec((tm, tn), lambda i, j: (i, off_ga // tn + j)),
                  pl.BlockSpec((tm, tn), lambda i, j: (i, off_gb // tn + j))],
        out_specs=pl.BlockSpec((tm, tn), lambda i, j: (i, j)),
        out_shape=jax.ShapeDtypeStruct((n, d), BF16),
        compiler_params=_params("parallel", "parallel"),
        name="merge",
    )(a_in, b_in, w_a, w_b, p, p)


def _residual(y, x, post_g, mod_ref, gate_row):
    return x + mod_ref[0, gate_row:gate_row + 1, :] * (_rms(y) * post_g)


def _outproj_kernel(m_ref, w_ref, x_ref, postg_ref, preg_ref, mod_ref, xo_ref, h_ref):
    y = _dot(m_ref[0], w_ref[...])
    x_new = _residual(y, x_ref[0], postg_ref[...], mod_ref, G1)
    xo_ref[0] = x_new
    h_ref[0] = _modulated_norm(x_new, preg_ref[...], mod_ref, SH2, SC2).astype(h_ref.dtype)


def _outproj(merged, w_o, x, post_g, pre_g, mod, h_dtype, tm=512):
    bsz, t, d = x.shape
    tm = min(tm, t)
    return pl.pallas_call(
        _outproj_kernel,
        grid=(bsz, t // tm),
        in_specs=[pl.BlockSpec((1, tm, d), lambda b, i: (b, i, 0)),
                  pl.BlockSpec((d, d), lambda b, i: (0, 0)),
                  pl.BlockSpec((1, tm, d), lambda b, i: (b, i, 0)),
                  pl.BlockSpec((1, d), lambda b, i: (0, 0)),
                  pl.BlockSpec((1, d), lambda b, i: (0, 0)),
                  pl.BlockSpec((1, 6, d), lambda b, i: (b, 0, 0))],
        out_specs=(pl.BlockSpec((1, tm, d), lambda b, i: (b, i, 0)),
                   pl.BlockSpec((1, tm, d), lambda b, i: (b, i, 0))),
        out_shape=(jax.ShapeDtypeStruct((bsz, t, d), F32),
                   jax.ShapeDtypeStruct((bsz, t, d), h_dtype)),
        compiler_params=_params("parallel", "parallel"),
        name="out_proj",
    )(merged.reshape(bsz, t, d), w_o, x, post_g.reshape(1, d), pre_g.reshape(1, d), mod)


def _ffn_res_kernel(*refs, with_next):
    if with_next:
        f_ref, x_ref, postg_ref, mod_ref, preg_ref, modn_ref, xo_ref, h_ref = refs
    else:
        f_ref, x_ref, postg_ref, mod_ref, xo_ref = refs
    x_new = _residual(f_ref[0], x_ref[0], postg_ref[...], mod_ref, G2)
    xo_ref[0] = x_new
    if with_next:
        h_ref[0] = _modulated_norm(x_new, preg_ref[...], modn_ref, SH1, SC1).astype(h_ref.dtype)


def _ffn_residual(f, x, post_g, mod, next_pre_g=None, next_mod=None, tt=256):
    bsz, t, d = x.shape
    with_next = next_pre_g is not None
    tile = pl.BlockSpec((1, tt, d), lambda b, i: (b, i, 0))
    vec = pl.BlockSpec((1, d), lambda b, i: (0, 0))
    modspec = pl.BlockSpec((1, 6, d), lambda b, i: (b, 0, 0))
    in_specs = [tile, tile, vec, modspec]
    args = [f.reshape(bsz, t, d), x, post_g.reshape(1, d), mod]
    out_shape = [jax.ShapeDtypeStruct((bsz, t, d), F32)]
    out_specs = [tile]
    if with_next:
        in_specs += [vec, modspec]
        args += [next_pre_g.reshape(1, d), next_mod]
        out_shape.append(jax.ShapeDtypeStruct((bsz, t, d), BF16))
        out_specs.append(tile)
    return pl.pallas_call(
        functools.partial(_ffn_res_kernel, with_next=with_next),
        grid=(bsz, t // tt),
        in_specs=in_specs,
        out_specs=tuple(out_specs),
        out_shape=tuple(out_shape),
        compiler_params=_params("parallel", "parallel"),
        name="ffn_residual",
    )(*args)


def _ffn_kernel(h_ref, w1_ref, w3_ref, w2_ref, o_ref):
    j = pl.program_id(1)
    h = h_ref[...]
    a = _dot(h, w1_ref[...].astype(BF16))
    b = _dot(h, w3_ref[...].astype(BF16))
    y = _dot((_silu(a) * b).astype(BF16), w2_ref[...].astype(BF16))

    @pl.when(j == 0)
    def _():
        o_ref[...] = y

    @pl.when(j > 0)
    def _():
        o_ref[...] += y


def _ffn(h, w1, w3, w2, layer_idx, tm=1024, tf=256):
    n, d = h.shape
    f = w1.shape[-1]
    tm = min(tm, n)
    return pl.pallas_call(
        _ffn_kernel,
        grid=(n // tm, f // tf),
        in_specs=[pl.BlockSpec((tm, d), lambda i, j: (i, 0)),
                  pl.BlockSpec((None, d, tf), lambda i, j: (layer_idx, 0, j)),
                  pl.BlockSpec((None, d, tf), lambda i, j: (layer_idx, 0, j)),
                  pl.BlockSpec((None, tf, d), lambda i, j: (layer_idx, j, 0))],
        out_specs=pl.BlockSpec((tm, d), lambda i, j: (i, 0)),
        out_shape=jax.ShapeDtypeStruct((n, d), F32),
        compiler_params=_params("parallel", "arbitrary"),
        name="ffn",
    )(h, w1, w3, w2)


def _router_kernel(h_ref, w_ref, o_ref, *, n_experts):
    h = h_ref[...]
    w = w_ref[...]
    h_hi = h.astype(BF16)
    h_lo = (h - h_hi.astype(F32)).astype(BF16)
    w_hi = w.astype(BF16)
    w_lo = (w - w_hi.astype(F32)).astype(BF16)
    logits = _dot(h_hi, w_hi) + (_dot(h_hi, w_lo) + _dot(h_lo, w_hi))
    lane = lax.broadcasted_iota(jnp.int32, logits.shape, 1)
    neg = -jnp.inf
    l1 = jnp.where(lane < n_experts, logits, neg)
    m1 = jnp.max(l1, axis=-1, keepdims=True)
    i1 = jnp.min(jnp.where(l1 == m1, lane, LANES), axis=-1, keepdims=True)
    l2 = jnp.where(lane == i1, neg, l1)
    m2 = jnp.max(l2, axis=-1, keepdims=True)
    i2 = jnp.min(jnp.where(l2 == m2, lane, LANES), axis=-1, keepdims=True)
    e = jnp.exp(m2 - m1)
    p1 = 1.0 / (1.0 + e)
    p2 = e / (1.0 + e)
    out = jnp.where(lane == 0, i1.astype(F32),
                    jnp.where(lane == 1, i2.astype(F32),
                              jnp.where(lane == 2, p1, jnp.where(lane == 3, p2, 0.0))))
    o_ref[...] = out


def _router(h, w_router, tm=512):
    n, d = h.shape
    n_experts = w_router.shape[1]
    w_pad = jnp.zeros((d, LANES), F32).at[:, :n_experts].set(w_router)
    return pl.pallas_call(
        functools.partial(_router_kernel, n_experts=n_experts),
        grid=(n // tm,),
        in_specs=[pl.BlockSpec((tm, d), lambda i: (i, 0)),
                  pl.BlockSpec((d, LANES), lambda i: (0, 0))],
        out_specs=pl.BlockSpec((tm, LANES), lambda i: (i, 0)),
        out_shape=jax.ShapeDtypeStruct((n, LANES), F32),
        compiler_params=_params("parallel"),
        name="router",
    )(h, w_pad)


def _invert_kernel(pos_ref, tok_ref, *, n_pairs, n_rows):
    def zero(r, carry):
        tok_ref[r] = 0
        return carry

    def scatter(p, carry):
        tok_ref[pos_ref[p]] = p // TOP_K
        return carry

    lax.fori_loop(0, n_rows, zero, 0)
    lax.fori_loop(0, n_pairs, scatter, 0)


def _invert(pos, n_rows):
    return pl.pallas_call(
        functools.partial(_invert_kernel, n_pairs=pos.shape[0], n_rows=n_rows),
        in_specs=[pl.BlockSpec(memory_space=pltpu.SMEM)],
        out_specs=pl.BlockSpec(memory_space=pltpu.SMEM),
        out_shape=jax.ShapeDtypeStruct((n_rows,), jnp.int32),
        name="moe_invert",
    )(pos)


def _dispatch_kernel(tok_ref, h_hbm, xg_ref, buf, sem, *, sub):
    def row_copy(r):
        return pltpu.make_async_copy(h_hbm.at[pl.ds(tok_ref[0, 0, r], 1)], buf.at[pl.ds(r, 1)], sem)

    def start(r, carry):
        row_copy(r).start()
        return carry

    def wait(r, carry):
        row_copy(r).wait()
        return carry

    lax.fori_loop(0, sub, start, 0)
    lax.fori_loop(0, sub, wait, 0)
    xg_ref[...] = buf[...].astype(xg_ref.dtype)


def _dispatch(h, row_tok, sub):
    n, d = h.shape
    n_rows = row_tok.shape[0]
    return pl.pallas_call(
        functools.partial(_dispatch_kernel, sub=sub),
        grid=(n_rows // sub,),
        in_specs=[pl.BlockSpec((1, 1, sub), lambda i: (i, 0, 0), memory_space=pltpu.SMEM),
                  pl.BlockSpec(memory_space=pl.ANY)],
        out_specs=pl.BlockSpec((sub, d), lambda i: (i, 0)),
        out_shape=jax.ShapeDtypeStruct((n_rows, d), BF16),
        scratch_shapes=[pltpu.VMEM((sub, d), F32), pltpu.SemaphoreType.DMA(())],
        compiler_params=_params("arbitrary"),
        name="moe_dispatch",
    )(row_tok.reshape(n_rows // sub, 1, sub), h)


def _moe_gate_kernel(exp_ref, first_ref, valid_ref, x_ref, w1_ref, w3_ref, g_ref, w1b, w3b):
    i = pl.program_id(1)

    @pl.when(first_ref[i] == 1)
    def _():
        w1b[...] = w1_ref[...].astype(BF16)
        w3b[...] = w3_ref[...].astype(BF16)

    @pl.when(valid_ref[i] == 1)
    def _():
        x = x_ref[...]
        a = _dot(x, w1b[...])
        b = _dot(x, w3b[...])
        g_ref[...] = (_silu(a) * b).astype(g_ref.dtype)

    @pl.when(valid_ref[i] == 0)
    def _():
        g_ref[...] = jnp.zeros_like(g_ref)


def _moe_down_kernel(exp_ref, first_ref, valid_ref, g_ref, w2_ref, y_ref, w2b):
    i = pl.program_id(1)

    @pl.when(first_ref[i] == 1)
    def _():
        w2b[...] = w2_ref[...].astype(BF16)

    @pl.when(valid_ref[i] == 1)
    def _():
        y_ref[...] = _dot(g_ref[...], w2b[...])

    @pl.when(valid_ref[i] == 0)
    def _():
        y_ref[...] = jnp.zeros_like(y_ref)


def _moe_experts(xg, tables, w1, w3, w2, moe_idx, sub, tf=512, tn=512):
    n_rows, d = xg.shape
    f = w1.shape[-1]
    n_items = n_rows // sub
    g = pl.pallas_call(
        _moe_gate_kernel,
        grid_spec=pltpu.PrefetchScalarGridSpec(
            num_scalar_prefetch=3,
            grid=(f // tf, n_items),
            in_specs=[pl.BlockSpec((sub, d), lambda j, i, exp, first, valid: (i, 0)),
                      pl.BlockSpec((None, None, d, tf), lambda j, i, exp, first, valid: (moe_idx, exp[i], 0, j)),
                      pl.BlockSpec((None, None, d, tf), lambda j, i, exp, first, valid: (moe_idx, exp[i], 0, j))],
            out_specs=pl.BlockSpec((sub, tf), lambda j, i, exp, first, valid: (i, j)),
            scratch_shapes=[pltpu.VMEM((d, tf), BF16), pltpu.VMEM((d, tf), BF16)]),
        out_shape=jax.ShapeDtypeStruct((n_rows, f), BF16),
        compiler_params=_params("arbitrary", "arbitrary"),
        name="moe_gate",
    )(*tables, xg, w1, w3)
    return pl.pallas_call(
        _moe_down_kernel,
        grid_spec=pltpu.PrefetchScalarGridSpec(
            num_scalar_prefetch=3,
            grid=(d // tn, n_items),
            in_specs=[pl.BlockSpec((sub, f), lambda j, i, exp, first, valid: (i, 0)),
                      pl.BlockSpec((None, None, f, tn), lambda j, i, exp, first, valid: (moe_idx, exp[i], 0, j))],
            out_specs=pl.BlockSpec((sub, tn), lambda j, i, exp, first, valid: (i, j)),
            scratch_shapes=[pltpu.VMEM((f, tn), BF16)]),
        out_shape=jax.ShapeDtypeStruct((n_rows, d), F32),
        compiler_params=_params("arbitrary", "arbitrary"),
        name="moe_down",
    )(*tables, g, w2)


def _combine_kernel(pos_ref, y_hbm, route_ref, x_ref, postg_ref, mod_ref, xo_ref, buf, sem, *, tt, tiles_per_batch):
    def row_copy(k, r):
        return pltpu.make_async_copy(y_hbm.at[pl.ds(pos_ref[0, 0, TOP_K * r + k], 1)],
                                     buf.at[k, pl.ds(r, 1)], sem)

    def start(r, carry):
        for k in range(TOP_K):
            row_copy(k, r).start()
        return carry

    def wait(r, carry):
        for k in range(TOP_K):
            row_copy(k, r).wait()
        return carry

    lax.fori_loop(0, tt, start, 0)
    lax.fori_loop(0, tt, wait, 0)
    route = route_ref[...]
    f = route[:, 2:3] * buf[0] + route[:, 3:4] * buf[1]
    xo_ref[0] = _residual(f, x_ref[0], postg_ref[...], mod_ref, G2)


def _combine(y, pos, route, x, post_g, mod, tt=256):
    bsz, t, d = x.shape
    n = bsz * t
    tpb = t // tt
    pos3 = pos.reshape(n // tt, 1, TOP_K * tt)
    return pl.pallas_call(
        functools.partial(_combine_kernel, tt=tt, tiles_per_batch=tpb),
        grid=(bsz, tpb),
        in_specs=[pl.BlockSpec((1, 1, TOP_K * tt), lambda b, i: (b * tpb + i, 0, 0), memory_space=pltpu.SMEM),
                  pl.BlockSpec(memory_space=pl.ANY),
                  pl.BlockSpec((tt, LANES), lambda b, i: (b * tpb + i, 0)),
                  pl.BlockSpec((1, tt, d), lambda b, i: (b, i, 0)),
                  pl.BlockSpec((1, d), lambda b, i: (0, 0)),
                  pl.BlockSpec((1, 6, d), lambda b, i: (b, 0, 0))],
        out_specs=pl.BlockSpec((1, tt, d), lambda b, i: (b, i, 0)),
        out_shape=jax.ShapeDtypeStruct((bsz, t, d), F32),
        scratch_shapes=[pltpu.VMEM((TOP_K, tt, d), F32), pltpu.SemaphoreType.DMA(())],
        compiler_params=_params("arbitrary", "arbitrary"),
        name="moe_combine",
    )(pos3, y, route, x, post_g.reshape(1, d), mod)


def _routing_tables(route, n_experts, sub, n_items):
    experts = route[:, :TOP_K].astype(jnp.int32).reshape(-1)
    onehot = (experts[:, None] == jnp.arange(n_experts, dtype=jnp.int32)[None, :]).astype(jnp.int32)
    csum = jnp.cumsum(onehot, axis=0)
    rank = jnp.sum(csum * onehot, axis=1) - 1
    counts = csum[-1]
    blocks = (counts + sub - 1) // sub
    block_end = jnp.cumsum(blocks)
    offs = (block_end - blocks) * sub
    pos = offs[experts] + rank
    n_valid = block_end[-1]
    item = jnp.arange(n_items, dtype=jnp.int32)
    blk = jnp.minimum(item, n_valid - 1)
    exp = jnp.minimum(jnp.sum((blk[:, None] >= block_end[None, :]).astype(jnp.int32), axis=1), n_experts - 1)
    valid = (item < n_valid).astype(jnp.int32)
    prev = jnp.concatenate([jnp.full((1,), -1, jnp.int32), exp[:-1]])
    first = ((exp != prev) & (valid == 1)).astype(jnp.int32)
    return pos.astype(jnp.int32), (exp, first, valid)


def _moe(h, x, router_w, w1, w3, w2, moe_idx, post_g, mod, sub=512):
    bsz, t, d = x.shape
    n = bsz * t
    n_experts = router_w.shape[-1]
    n_items = (n * TOP_K) // sub + n_experts
    hf = h.reshape(n, d)
    route = _router(hf, router_w[moe_idx])
    pos, tables = _routing_tables(route, n_experts, sub, n_items)
    xg = _dispatch(hf, _invert(pos, n_items * sub), sub)
    y = _moe_experts(xg, tables, w1, w3, w2, moe_idx, sub)
    return _combine(y, pos, route, x, post_g, mod)


def kernel(x, c, ctx, c_ctx, w_ada, b_ada, pre_mix_g, post_mix_g, pre_ffn_g, post_ffn_g, w_in, gate_w2_f, gate_b_f, gate_w2_b, gate_b_b, gla_norm_g, conv_w, conv_b, w_proj_a, w_proj_b, w_out, ffn_w1, ffn_w3, ffn_w2, router_w, moe_w1, moe_w3, moe_w2):
    bsz, seq, d = x.shape
    ctx_len = ctx.shape[1]
    depth = w_ada.shape[0]
    dk = gate_w2_f.shape[-1]
    dv = gla_norm_g.shape[-1]
    hk, hv = dk // N_HEADS, dv // N_HEADS
    off_lr = 2 * dk + dv
    n_lr = 2 * GATE_RANK
    ctx_cols = 2 * dk + dv
    off_r = ctx_cols
    off_cb = off_r + dv
    off_ga = off_cb + 3 * d
    off_gb = off_ga + d
    n_main = off_gb + d

    cc = jnp.concatenate([c, c_ctx[None, :], jnp.zeros((8 - bsz - 1, d), F32)], axis=0)
    mod_all = _ada(cc, w_ada, b_ada)

    xc = ctx
    hx = hc = None
    for layer in range(depth):
        last = layer == depth - 1
        mod_x = mod_all[layer, :bsz].reshape(bsz, 6, d)
        mod_c = jnp.broadcast_to(mod_all[layer, bsz].reshape(1, 6, d), (bsz, 6, d))

        w_in_l = w_in[layer]
        w_main = jnp.concatenate([w_in_l[:, :off_lr], w_in_l[:, off_lr + n_lr:]], axis=1).astype(BF16)
        w_lr = jnp.zeros((d, LANES), BF16).at[:, :n_lr].set(w_in_l[:, off_lr:off_lr + n_lr].astype(BF16))
        w2 = jnp.zeros((LANES, 2 * dk), BF16)
        w2 = w2.at[:GATE_RANK, :dk].set(gate_w2_f[layer].astype(BF16))
        w2 = w2.at[GATE_RANK:n_lr, dk:].set(gate_w2_b[layer].astype(BF16))
        gate_b = jnp.concatenate([gate_b_f[layer], gate_b_b[layer]]).reshape(1, 2 * dk)
        w_a = w_proj_a[layer].astype(BF16)
        w_b = w_proj_b[layer].astype(BF16)
        w_o = w_out[layer].astype(BF16)

        if hx is None:
            hx = _prenorm(x, pre_mix_g[layer], mod_x)
            hc = _prenorm(xc, pre_mix_g[layer], mod_c)

        hc2 = hc.reshape(bsz * ctx_len, d)
        pc = _matmul(hc2, w_main, ctx_cols if last else n_main).reshape(bsz, ctx_len, -1)
        la_c = _decay(hc2, w_lr, w2, gate_b).reshape(bsz, ctx_len, 2 * dk)
        s0 = jnp.zeros((bsz, N_HEADS, hv, hk), F32)
        if last:
            s_f, s_b = _gla(pc, la_c, s0, s0, None, dk, dv, off_r, want_out=False)
        else:
            o_c, s_f, s_b = _gla(pc, la_c, s0, s0, gla_norm_g[layer], dk, dv, off_r, want_out=True)

        hx2 = hx.reshape(bsz * seq, d)
        px = _matmul(hx2, w_main, n_main).reshape(bsz, seq, n_main)
        la_x = _decay(hx2, w_lr, w2, gate_b).reshape(bsz, seq, 2 * dk)
        o_x, _, _ = _gla(px, la_x, s_f, s_b, gla_norm_g[layer], dk, dv, off_r, want_out=True)
        a_x = _gated_conv(px, conv_w[layer], conv_b[layer], d, off_cb, seg=GRID_W)
        merged = _merge(a_x.reshape(bsz * seq, d), o_x.reshape(bsz * seq, d), w_a, w_b,
                        px.reshape(bsz * seq, n_main), off_ga, off_gb)
        moe_layer = layer % 2 == 1
        x, h2 = _outproj(merged, w_o, x, post_mix_g[layer], pre_ffn_g[layer], mod_x,
                         F32 if moe_layer else BF16)
        nxt = None if last else (pre_mix_g[layer + 1], mod_all[layer + 1, :bsz].reshape(bsz, 6, d))
        if moe_layer:
            x_new = _moe(h2, x, router_w, moe_w1, moe_w3, moe_w2, layer // 2, post_ffn_g[layer], mod_x)
            if nxt is None:
                x = x_new
            else:
                x = x_new
                hx = _prenorm(x, nxt[0], nxt[1])
        else:
            f_x = _ffn(h2.reshape(bsz * seq, d), ffn_w1, ffn_w3, ffn_w2, layer // 2)
            if nxt is None:
                (x,) = _ffn_residual(f_x, x, post_ffn_g[layer], mod_x)
            else:
                x, hx = _ffn_residual(f_x, x, post_ffn_g[layer], mod_x, nxt[0], nxt[1])

        if not last:
            a_c = _gated_conv(pc, conv_w[layer], conv_b[layer], d, off_cb, seg=ctx_len, tt=ctx_len)
            merged_c = _merge(a_c.reshape(bsz * ctx_len, d), o_c.reshape(bsz * ctx_len, d), w_a, w_b,
                              pc.reshape(bsz * ctx_len, n_main), off_ga, off_gb)
            xc, h2c = _outproj(merged_c, w_o, xc, post_mix_g[layer], pre_ffn_g[layer], mod_c,
                               F32 if moe_layer else BF16)
            mod_cn = jnp.broadcast_to(mod_all[layer + 1, bsz].reshape(1, 6, d), (bsz, 6, d))
            if moe_layer:
                xc = _moe(h2c, xc, router_w, moe_w1, moe_w3, moe_w2, layer // 2, post_ffn_g[layer], mod_c)
                hc = _prenorm(xc, pre_mix_g[layer + 1], mod_cn)
            else:
                f_c = _ffn(h2c.reshape(bsz * ctx_len, d), ffn_w1, ffn_w3, ffn_w2, layer // 2)
                xc, hc = _ffn_residual(f_c, xc, post_ffn_g[layer], mod_c, pre_mix_g[layer + 1], mod_cn)
    return x
```

```python
import functools

import jax
import jax.numpy as jnp
from jax import lax
from jax.experimental import pallas as pl
from jax.experimental.pallas import tpu as pltpu

F32 = jnp.float32
BF16 = jnp.bfloat16

EPS = 1e-6
N_HEADS = 4
GATE_RANK = 16
GATE_TAU = 16.0
GLA_CHUNK = 64
GRID_W = 64
TOP_K = 2

LANES = 128
VMEM_LIMIT = 56 * 1024 * 1024

SH1, SC1, G1, SH2, SC2, G2 = range(6)


def _params(*sem):
    return pltpu.CompilerParams(dimension_semantics=sem, vmem_limit_bytes=VMEM_LIMIT)


def _dot(a, b):
    return jnp.dot(a, b, preferred_element_type=F32)


def _dot_nt(a, b):
    return lax.dot_general(a, b, (((1,), (1,)), ((), ())), preferred_element_type=F32)


def _dot_tn(a, b):
    return lax.dot_general(a, b, (((0,), (0,)), ((), ())), preferred_element_type=F32)


def _silu(x):
    return x * jax.nn.sigmoid(x)


def _rms(x):
    return x * lax.rsqrt(jnp.mean(x * x, axis=-1, keepdims=True) + EPS)


def _ada_kernel(c_ref, w_ref, b_ref, o_ref):
    s = _silu(c_ref[...]).astype(BF16)
    o_ref[0] = _dot(s, w_ref[0].astype(BF16)) + b_ref[0]


def _ada(cc, w_ada, b_ada, tn=1024):
    depth, d, n = w_ada.shape
    rows = cc.shape[0]
    return pl.pallas_call(
        _ada_kernel,
        grid=(depth, n // tn),
        in_specs=[pl.BlockSpec((rows, d), lambda l, j: (0, 0)),
                  pl.BlockSpec((1, d, tn), lambda l, j: (l, 0, j)),
                  pl.BlockSpec((1, 1, tn), lambda l, j: (l, 0, j))],
        out_specs=pl.BlockSpec((1, rows, tn), lambda l, j: (l, 0, j)),
        out_shape=jax.ShapeDtypeStruct((depth, rows, n), F32),
        compiler_params=_params("parallel", "parallel"),
        name="ada",
    )(cc, w_ada, b_ada.reshape(depth, 1, n))


def _modulated_norm(x, g, mod_ref, shift_row, scale_row):
    h = _rms(x) * g
    return h * (1.0 + mod_ref[0, scale_row:scale_row + 1, :]) + mod_ref[0, shift_row:shift_row + 1, :]


def _prenorm_kernel(x_ref, g_ref, mod_ref, o_ref):
    o_ref[0] = _modulated_norm(x_ref[0], g_ref[...], mod_ref, SH1, SC1).astype(o_ref.dtype)


def _prenorm(x, g, mod, tt=256):
    b, t, d = x.shape
    return pl.pallas_call(
        _prenorm_kernel,
        grid=(b, t // tt),
        in_specs=[pl.BlockSpec((1, tt, d), lambda i, j: (i, j, 0)),
                  pl.BlockSpec((1, d), lambda i, j: (0, 0)),
                  pl.BlockSpec((1, 6, d), lambda i, j: (i, 0, 0))],
        out_specs=pl.BlockSpec((1, tt, d), lambda i, j: (i, j, 0)),
        out_shape=jax.ShapeDtypeStruct((b, t, d), BF16),
        compiler_params=_params("parallel", "parallel"),
        name="prenorm",
    )(x, g.reshape(1, d), mod)


def _mm_kernel(h_ref, w_ref, o_ref):
    o_ref[...] = _dot(h_ref[...], w_ref[...])


def _matmul(h, w, ncols, tm=1024, tn=1024):
    n, k = h.shape
    tm = min(tm, n)
    return pl.pallas_call(
        _mm_kernel,
        grid=(n // tm, ncols // tn),
        in_specs=[pl.BlockSpec((tm, k), lambda i, j: (i, 0)),
                  pl.BlockSpec((k, tn), lambda i, j: (0, j))],
        out_specs=pl.BlockSpec((tm, tn), lambda i, j: (i, j)),
        out_shape=jax.ShapeDtypeStruct((n, ncols), F32),
        compiler_params=_params("parallel", "parallel"),
        name="in_proj",
    )(h, w)


def _decay_kernel(h_ref, wlr_ref, w2_ref, b_ref, o_ref):
    lr = _dot(h_ref[...], wlr_ref[...]).astype(BF16)
    z = _dot(lr, w2_ref[...]) + b_ref[...]
    log_sig = jnp.minimum(z, 0.0) - jnp.log1p(jnp.exp(-jnp.abs(z)))
    o_ref[...] = log_sig / GATE_TAU


def _decay(h, w_lr, w2, bias, tm=512):
    n, k = h.shape
    r = w_lr.shape[1]
    m = w2.shape[1]
    return pl.pallas_call(
        _decay_kernel,
        grid=(n // tm,),
        in_specs=[pl.BlockSpec((tm, k), lambda i: (i, 0)),
                  pl.BlockSpec((k, r), lambda i: (0, 0)),
                  pl.BlockSpec((r, m), lambda i: (0, 0)),
                  pl.BlockSpec((1, m), lambda i: (0, 0))],
        out_specs=pl.BlockSpec((tm, m), lambda i: (i, 0)),
        out_shape=jax.ShapeDtypeStruct((n, m), F32),
        compiler_params=_params("parallel"),
        name="decay",
    )(h, w_lr, w2, bias)


def _gla_kernel(*refs, t, scale, want_out):
    if want_out:
        (q_ref, k_ref, v_ref, laf_ref, lab_ref, s0f_ref, s0b_ref, r_ref, g_ref,
         o_ref, sf_ref, sb_ref,
         of_scr, ob_scr, qtf_scr, qtb_scr, kef_scr, keb_scr, df_scr, db_scr, stf_scr, stb_scr) = refs
    else:
        (q_ref, k_ref, v_ref, laf_ref, lab_ref, s0f_ref, s0b_ref,
         sf_ref, sb_ref, kef_scr, keb_scr, df_scr, db_scr, stf_scr, stb_scr) = refs
        of_scr = ob_scr = qtf_scr = qtb_scr = None
    c_len = GLA_CHUNK
    n_chunks = t // c_len
    blk_len = min(4 * c_len, t)
    n_blocks = t // blk_len
    per_blk = blk_len // c_len
    hk = k_ref.shape[-1]
    row = lax.broadcasted_iota(jnp.int32, (blk_len, blk_len), 0)
    col = lax.broadcasted_iota(jnp.int32, (blk_len, blk_len), 1)
    same_chunk = (row // c_len) == (col // c_len)
    mask_f = same_chunk & (row >= col)
    mask_b = same_chunk & (row <= col)
    tri_f = mask_f.astype(BF16)
    tri_b = mask_b.astype(BF16)

    def prepare(blk, la_ref, tri, mask, last_row, qt_scr, ke_scr, d_scr, o_scr):
        sl = pl.ds(pl.multiple_of(blk * blk_len, blk_len), blk_len)
        la = la_ref[0, sl, :]
        la_hi = la.astype(BF16)
        la_lo = (la - la_hi.astype(F32)).astype(BF16)
        b = _dot(tri, la_hi) + _dot(tri, la_lo)
        b_last = jnp.concatenate(
            [jnp.broadcast_to(b[i * c_len + last_row:i * c_len + last_row + 1, :], (c_len, hk))
             for i in range(per_blk)], axis=0)
        k = k_ref[0, sl, :]
        ke_scr[sl, :] = (k * jnp.exp(b_last - b)).astype(BF16)
        d = jnp.exp(b_last)
        for i in range(per_blk):
            d_scr[pl.ds(pl.multiple_of((blk * per_blk + i) * 8, 8), 8), :] = d[i * c_len:i * c_len + 8, :]
        if want_out:
            qt = (q_ref[0, sl, :] * scale * jnp.exp(b)).astype(BF16)
            qt_scr[sl, :] = qt
            kt = (k * jnp.exp(-b)).astype(BF16)
            att = jnp.where(mask, _dot_nt(qt, kt), 0.0).astype(BF16)
            o_scr[sl, :] = _dot(att, v_ref[0, sl, :].astype(BF16))

    def prepare_body(blk, carry):
        prepare(blk, laf_ref, tri_f, mask_f, c_len - 1, qtf_scr, kef_scr, df_scr, of_scr)
        prepare(blk, lab_ref, tri_b, mask_b, 0, qtb_scr, keb_scr, db_scr, ob_scr)
        return carry

    lax.fori_loop(0, n_blocks, prepare_body, 0)

    stf_scr[...] = s0f_ref[0, 0]
    stb_scr[...] = s0b_ref[0, 0]

    def scan_step(c, qt_scr, ke_scr, d_scr, o_scr, st_scr):
        sl = pl.ds(pl.multiple_of(c * c_len, c_len), c_len)
        st = st_scr[...]
        if want_out:
            o_scr[sl, :] += _dot_nt(qt_scr[sl, :], st.astype(BF16))
        d = d_scr[pl.ds(pl.multiple_of(c * 8, 8), 8), :]
        st_scr[...] = d[0:1, :] * st + _dot_tn(v_ref[0, sl, :].astype(BF16), ke_scr[sl, :])

    def scan_body(c, carry):
        scan_step(c, qtf_scr, kef_scr, df_scr, of_scr, stf_scr)
        scan_step(n_chunks - 1 - c, qtb_scr, keb_scr, db_scr, ob_scr, stb_scr)
        return carry

    lax.fori_loop(0, n_chunks, scan_body, 0, unroll=2)
    sf_ref[0, 0] = stf_scr[...]
    sb_ref[0, 0] = stb_scr[...]

    if want_out:
        def epilogue(c, carry):
            sl = pl.ds(pl.multiple_of(c * c_len, c_len), c_len)
            o = _rms(of_scr[sl, :] + ob_scr[sl, :]) * g_ref[...]
            o_ref[0, sl, :] = (o * _silu(r_ref[0, sl, :])).astype(o_ref.dtype)
            return carry

        lax.fori_loop(0, n_chunks, epilogue, 0)


def _gla(p, p_r, la, s0f, s0b, gla_g, dk_total, dv_total, off_r, want_out):
    bsz, t, _ = p.shape
    hk = dk_total // N_HEADS
    hv = dv_total // N_HEADS
    nkb = dk_total // hk
    kern = functools.partial(_gla_kernel, t=t, scale=hk ** -0.5, want_out=want_out)
    n_chunks = t // GLA_CHUNK
    in_specs = [
        pl.BlockSpec((1, t, hk), lambda b, h: (b, 0, h)),
        pl.BlockSpec((1, t, hk), lambda b, h: (b, 0, nkb + h)),
        pl.BlockSpec((1, t, hv), lambda b, h: (b, 0, 2 * dk_total // hv + h)),
        pl.BlockSpec((1, t, hk), lambda b, h: (b, 0, h)),
        pl.BlockSpec((1, t, hk), lambda b, h: (b, 0, nkb + h)),
        pl.BlockSpec((1, 1, hv, hk), lambda b, h: (b, h, 0, 0)),
        pl.BlockSpec((1, 1, hv, hk), lambda b, h: (b, h, 0, 0)),
    ]
    args = [p, p, p, la, la, s0f, s0b]
    st_shape = jax.ShapeDtypeStruct((bsz, N_HEADS, hv, hk), F32)
    st_spec = pl.BlockSpec((1, 1, hv, hk), lambda b, h: (b, h, 0, 0))
    scratch = [pltpu.VMEM((t, hk), BF16), pltpu.VMEM((t, hk), BF16),
               pltpu.VMEM((n_chunks * 8, hk), F32), pltpu.VMEM((n_chunks * 8, hk), F32),
               pltpu.VMEM((hv, hk), F32), pltpu.VMEM((hv, hk), F32)]
    if want_out:
        in_specs += [pl.BlockSpec((1, t, hv), lambda b, h: (b, 0, off_r // hv + h)),
                     pl.BlockSpec((1, hv), lambda b, h: (0, h))]
        args += [p_r, gla_g.reshape(1, dv_total)]
        out_shape = (jax.ShapeDtypeStruct((bsz, t, dv_total), BF16), st_shape, st_shape)
        out_specs = (pl.BlockSpec((1, t, hv), lambda b, h: (b, 0, h)), st_spec, st_spec)
        scratch = [pltpu.VMEM((t, hv), F32), pltpu.VMEM((t, hv), F32),
                   pltpu.VMEM((t, hk), BF16), pltpu.VMEM((t, hk), BF16)] + scratch
    else:
        out_shape = (st_shape, st_shape)
        out_specs = (st_spec, st_spec)
    return pl.pallas_call(
        kern,
        grid=(bsz, N_HEADS),
        in_specs=in_specs,
        out_specs=out_specs,
        out_shape=out_shape,
        scratch_shapes=scratch,
        compiler_params=_params("parallel", "parallel"),
        name="gla",
    )(*args)


def _conv_kernel(cb_ref, cc_ref, cx_ref, w_ref, b_ref, o_ref, *, seg):
    u = cc_ref[0] * cx_ref[0]
    tt = u.shape[0]
    pos = lax.broadcasted_iota(jnp.int32, u.shape, 0) % seg
    u_prev = jnp.where(pos == 0, 0.0, pltpu.roll(u, 1, axis=0))
    u_next = jnp.where(pos == seg - 1, 0.0, pltpu.roll(u, tt - 1, axis=0))
    conv = b_ref[...] + u_prev * w_ref[0:1, :]
    conv = conv + u * w_ref[1:2, :]
    conv = conv + u_next * w_ref[2:3, :]
    o_ref[0] = (cb_ref[0] * conv).astype(o_ref.dtype)


def _gated_conv(p, conv_w, conv_b, d, off_cb, seg, tt=256):
    bsz, t, _ = p.shape
    cb = off_cb // d
    kern = functools.partial(_conv_kernel, seg=seg)
    return pl.pallas_call(
        kern,
        grid=(bsz, t // tt),
        in_specs=[pl.BlockSpec((1, tt, d), lambda b, i: (b, i, cb)),
                  pl.BlockSpec((1, tt, d), lambda b, i: (b, i, cb + 1)),
                  pl.BlockSpec((1, tt, d), lambda b, i: (b, i, cb + 2)),
                  pl.BlockSpec((3, d), lambda b, i: (0, 0)),
                  pl.BlockSpec((1, d), lambda b, i: (0, 0))],
        out_specs=pl.BlockSpec((1, tt, d), lambda b, i: (b, i, 0)),
        out_shape=jax.ShapeDtypeStruct((bsz, t, d), BF16),
        compiler_params=_params("parallel", "parallel"),
        name="gated_conv",
    )(p, p, p, conv_w, conv_b.reshape(1, d))


def _merge_kernel(a_ref, b_ref, wa_ref, wb_ref, ga_ref, gb_ref, o_ref):
    y_a = _dot(a_ref[...], wa_ref[...])
    y_b = _dot(b_ref[...], wb_ref[...])
    o_ref[...] = (jax.nn.sigmoid(ga_ref[...]) * y_a + jax.nn.sigmoid(gb_ref[...]) * y_b).astype(o_ref.dtype)


def _merge(a_in, b_in, w_a, w_b, p, off_ga, off_gb, tm=1024, tn=512):
    n, d = a_in.shape
    tm = min(tm, n)
    return pl.pallas_call(
        _merge_kernel,
        grid=(n // tm, d // tn),
        in_specs=[pl.BlockSpec((tm, d), lambda i, j: (i, 0)),
                  pl.BlockSpec((tm, d), lambda i, j: (i, 0)),
                  pl.BlockSpec((d, tn), lambda i, j: (0, j)),
                  pl.BlockSpec((d, tn), lambda i, j: (0, j)),
                  pl.BlockSpec((tm, tn), lambda i, j: (i, off_ga // tn + j)),
                  pl.BlockSpec((tm, tn), lambda i, j: (i, off_gb // tn + j))],
        out_specs=pl.BlockSpec((tm, tn), lambda i, j: (i, j)),
        out_shape=jax.ShapeDtypeStruct((n, d), BF16),
        compiler_params=_params("parallel", "parallel"),
        name="merge",
    )(a_in, b_in, w_a, w_b, p, p)


def _residual(y, x, post_g, mod_ref, gate_row):
    return x + mod_ref[0, gate_row:gate_row + 1, :] * (_rms(y) * post_g)


def _outproj_kernel(m_ref, w_ref, x_ref, postg_ref, preg_ref, mod_ref, xo_ref, h_ref):
    y = _dot(m_ref[0], w_ref[...])
    x_new = _residual(y, x_ref[0], postg_ref[...], mod_ref, G1)
    xo_ref[0] = x_new
    h_ref[0] = _modulated_norm(x_new, preg_ref[...], mod_ref, SH2, SC2).astype(h_ref.dtype)


def _outproj(merged, w_o, x, post_g, pre_g, mod, h_dtype, tm=512):
    bsz, t, d = x.shape
    tm = min(tm, t)
    return pl.pallas_call(
        _outproj_kernel,
        grid=(bsz, t // tm),
        in_specs=[pl.BlockSpec((1, tm, d), lambda b, i: (b, i, 0)),
                  pl.BlockSpec((d, d), lambda b, i: (0, 0)),
                  pl.BlockSpec((1, tm, d), lambda b, i: (b, i, 0)),
                  pl.BlockSpec((1, d), lambda b, i: (0, 0)),
                  pl.BlockSpec((1, d), lambda b, i: (0, 0)),
                  pl.BlockSpec((1, 6, d), lambda b, i: (b, 0, 0))],
        out_specs=(pl.BlockSpec((1, tm, d), lambda b, i: (b, i, 0)),
                   pl.BlockSpec((1, tm, d), lambda b, i: (b, i, 0))),
        out_shape=(jax.ShapeDtypeStruct((bsz, t, d), F32),
                   jax.ShapeDtypeStruct((bsz, t, d), h_dtype)),
        compiler_params=_params("parallel", "parallel"),
        name="out_proj",
    )(merged.reshape(bsz, t, d), w_o, x, post_g.reshape(1, d), pre_g.reshape(1, d), mod)


def _ffn_res_kernel(*refs, with_next):
    if with_next:
        f_ref, x_ref, postg_ref, mod_ref, preg_ref, modn_ref, xo_ref, h_ref = refs
    else:
        f_ref, x_ref, postg_ref, mod_ref, xo_ref = refs
    x_new = _residual(f_ref[0], x_ref[0], postg_ref[...], mod_ref, G2)
    xo_ref[0] = x_new
    if with_next:
        h_ref[0] = _modulated_norm(x_new, preg_ref[...], modn_ref, SH1, SC1).astype(h_ref.dtype)


def _ffn_residual(f, x, post_g, mod, next_pre_g=None, next_mod=None, tt=256):
    bsz, t, d = x.shape
    with_next = next_pre_g is not None
    tile = pl.BlockSpec((1, tt, d), lambda b, i: (b, i, 0))
    vec = pl.BlockSpec((1, d), lambda b, i: (0, 0))
    modspec = pl.BlockSpec((1, 6, d), lambda b, i: (b, 0, 0))
    in_specs = [tile, tile, vec, modspec]
    args = [f.reshape(bsz, t, d), x, post_g.reshape(1, d), mod]
    out_shape = [jax.ShapeDtypeStruct((bsz, t, d), F32)]
    out_specs = [tile]
    if with_next:
        in_specs += [vec, modspec]
        args += [next_pre_g.reshape(1, d), next_mod]
        out_shape.append(jax.ShapeDtypeStruct((bsz, t, d), BF16))
        out_specs.append(tile)
    return pl.pallas_call(
        functools.partial(_ffn_res_kernel, with_next=with_next),
        grid=(bsz, t // tt),
        in_specs=in_specs,
        out_specs=tuple(out_specs),
        out_shape=tuple(out_shape),
        compiler_params=_params("parallel", "parallel"),
        name="ffn_residual",
    )(*args)


def _ffn_gate_kernel(h_ref, w1_ref, w3_ref, g_ref):
    h = h_ref[...]
    a = _dot(h, w1_ref[...].astype(BF16))
    b = _dot(h, w3_ref[...].astype(BF16))
    g_ref[...] = (_silu(a) * b).astype(g_ref.dtype)


def _ffn_down_kernel(g_ref, w2_ref, o_ref):
    o_ref[...] = _dot(g_ref[...], w2_ref[...].astype(BF16))


def _ffn(h, w1, w3, w2, layer_idx, tm=1024, tf=512, tn=256):
    n, d = h.shape
    f = w1.shape[-1]
    tm = min(tm, n)
    g = pl.pallas_call(
        _ffn_gate_kernel,
        grid=(n // tm, f // tf),
        in_specs=[pl.BlockSpec((tm, d), lambda i, j: (i, 0)),
                  pl.BlockSpec((None, d, tf), lambda i, j: (layer_idx, 0, j)),
                  pl.BlockSpec((None, d, tf), lambda i, j: (layer_idx, 0, j))],
        out_specs=pl.BlockSpec((tm, tf), lambda i, j: (i, j)),
        out_shape=jax.ShapeDtypeStruct((n, f), BF16),
        compiler_params=_params("parallel", "parallel"),
        name="ffn_gate",
    )(h, w1, w3)
    return pl.pallas_call(
        _ffn_down_kernel,
        grid=(n // tm, d // tn),
        in_specs=[pl.BlockSpec((tm, f), lambda i, j: (i, 0)),
                  pl.BlockSpec((None, f, tn), lambda i, j: (layer_idx, 0, j))],
        out_specs=pl.BlockSpec((tm, tn), lambda i, j: (i, j)),
        out_shape=jax.ShapeDtypeStruct((n, d), F32),
        compiler_params=_params("parallel", "parallel"),
        name="ffn_down",
    )(g, w2)


def _router_kernel(h_ref, w_ref, o_ref, *, n_experts):
    h = h_ref[...]
    w = w_ref[...]
    h_hi = h.astype(BF16)
    h_lo = (h - h_hi.astype(F32)).astype(BF16)
    w_hi = w.astype(BF16)
    w_lo = (w - w_hi.astype(F32)).astype(BF16)
    logits = _dot(h_hi, w_hi) + (_dot(h_hi, w_lo) + _dot(h_lo, w_hi))
    lane = lax.broadcasted_iota(jnp.int32, logits.shape, 1)
    neg = -jnp.inf
    l1 = jnp.where(lane < n_experts, logits, neg)
    m1 = jnp.max(l1, axis=-1, keepdims=True)
    i1 = jnp.min(jnp.where(l1 == m1, lane, LANES), axis=-1, keepdims=True)
    l2 = jnp.where(lane == i1, neg, l1)
    m2 = jnp.max(l2, axis=-1, keepdims=True)
    i2 = jnp.min(jnp.where(l2 == m2, lane, LANES), axis=-1, keepdims=True)
    e = jnp.exp(m2 - m1)
    p1 = 1.0 / (1.0 + e)
    p2 = e / (1.0 + e)
    out = jnp.where(lane == 0, i1.astype(F32),
                    jnp.where(lane == 1, i2.astype(F32),
                              jnp.where(lane == 2, p1, jnp.where(lane == 3, p2, 0.0))))
    o_ref[...] = out


def _router(h, w_router, tm=512):
    n, d = h.shape
    n_experts = w_router.shape[1]
    w_pad = jnp.zeros((d, LANES), F32).at[:, :n_experts].set(w_router)
    return pl.pallas_call(
        functools.partial(_router_kernel, n_experts=n_experts),
        grid=(n // tm,),
        in_specs=[pl.BlockSpec((tm, d), lambda i: (i, 0)),
                  pl.BlockSpec((d, LANES), lambda i: (0, 0))],
        out_specs=pl.BlockSpec((tm, LANES), lambda i: (i, 0)),
        out_shape=jax.ShapeDtypeStruct((n, LANES), F32),
        compiler_params=_params("parallel"),
        name="router",
    )(h, w_pad)


def _invert_kernel(pos_ref, tok_ref, *, n_pairs, n_rows):
    def zero(r, carry):
        tok_ref[r] = 0
        return carry

    def scatter(p, carry):
        tok_ref[pos_ref[p]] = p // TOP_K
        return carry

    lax.fori_loop(0, n_rows, zero, 0, unroll=16)
    lax.fori_loop(0, n_pairs, scatter, 0, unroll=16)


def _invert(pos, n_rows):
    return pl.pallas_call(
        functools.partial(_invert_kernel, n_pairs=pos.shape[0], n_rows=n_rows),
        in_specs=[pl.BlockSpec(memory_space=pltpu.SMEM)],
        out_specs=pl.BlockSpec(memory_space=pltpu.SMEM),
        out_shape=jax.ShapeDtypeStruct((n_rows,), jnp.int32),
        name="moe_invert",
    )(pos)


def _dispatch_kernel(tok_ref, tok_next_ref, h_hbm, xg_ref, buf, sem, *, sub):
    i = pl.program_id(0)
    slot = i % 2

    def row_copy(idx_ref, s, r):
        return pltpu.make_async_copy(h_hbm.at[pl.ds(idx_ref[0, 0, r], 1)], buf.at[s, pl.ds(r, 1)], sem.at[s])

    def start_all(idx_ref, s):
        def start(r, carry):
            row_copy(idx_ref, s, r).start()
            return carry
        lax.fori_loop(0, sub, start, 0, unroll=8)

    @pl.when(i == 0)
    def _():
        start_all(tok_ref, 0)

    @pl.when(i + 1 < pl.num_programs(0))
    def _():
        start_all(tok_next_ref, 1 - slot)

    def wait(r, carry):
        row_copy(tok_ref, slot, r).wait()
        return carry

    lax.fori_loop(0, sub, wait, 0, unroll=8)
    xg_ref[...] = buf[slot].astype(xg_ref.dtype)


def _dispatch(h, row_tok, sub):
    n, d = h.shape
    n_rows = row_tok.shape[0]
    n_blocks = n_rows // sub
    tok3 = row_tok.reshape(n_blocks, 1, sub)
    return pl.pallas_call(
        functools.partial(_dispatch_kernel, sub=sub),
        grid=(n_blocks,),
        in_specs=[pl.BlockSpec((1, 1, sub), lambda i: (i, 0, 0), memory_space=pltpu.SMEM),
                  pl.BlockSpec((1, 1, sub), lambda i: (jnp.minimum(i + 1, n_blocks - 1), 0, 0),
                               memory_space=pltpu.SMEM),
                  pl.BlockSpec(memory_space=pl.ANY)],
        out_specs=pl.BlockSpec((sub, d), lambda i: (i, 0)),
        out_shape=jax.ShapeDtypeStruct((n_rows, d), BF16),
        scratch_shapes=[pltpu.VMEM((2, sub, d), F32), pltpu.SemaphoreType.DMA((2,))],
        compiler_params=_params("arbitrary"),
        name="moe_dispatch",
    )(tok3, tok3, h)


def _moe_gate_kernel(exp_ref, first_ref, valid_ref, x_ref, w1_ref, w3_ref, g_ref, w1b, w3b):
    i = pl.program_id(1)

    @pl.when(first_ref[i] == 1)
    def _():
        w1b[...] = w1_ref[...].astype(BF16)
        w3b[...] = w3_ref[...].astype(BF16)

    @pl.when(valid_ref[i] == 1)
    def _():
        x = x_ref[...]
        a = _dot(x, w1b[...])
        b = _dot(x, w3b[...])
        g_ref[...] = (_silu(a) * b).astype(g_ref.dtype)

    @pl.when(valid_ref[i] == 0)
    def _():
        g_ref[...] = jnp.zeros_like(g_ref)


def _moe_down_kernel(exp_ref, first_ref, valid_ref, g_ref, w2_ref, y_ref, w2b):
    i = pl.program_id(1)

    @pl.when(first_ref[i] == 1)
    def _():
        w2b[...] = w2_ref[...].astype(BF16)

    @pl.when(valid_ref[i] == 1)
    def _():
        y_ref[...] = _dot(g_ref[...], w2b[...])

    @pl.when(valid_ref[i] == 0)
    def _():
        y_ref[...] = jnp.zeros_like(y_ref)


def _moe_experts(xg, tables, w1, w3, w2, moe_idx, sub, tf=512, tn=512):
    n_rows, d = xg.shape
    f = w1.shape[-1]
    n_items = n_rows // sub
    g = pl.pallas_call(
        _moe_gate_kernel,
        grid_spec=pltpu.PrefetchScalarGridSpec(
            num_scalar_prefetch=3,
            grid=(f // tf, n_items),
            in_specs=[pl.BlockSpec((sub, d), lambda j, i, exp, first, valid: (i, 0)),
                      pl.BlockSpec((None, None, d, tf), lambda j, i, exp, first, valid: (moe_idx, exp[i], 0, j)),
                      pl.BlockSpec((None, None, d, tf), lambda j, i, exp, first, valid: (moe_idx, exp[i], 0, j))],
            out_specs=pl.BlockSpec((sub, tf), lambda j, i, exp, first, valid: (i, j)),
            scratch_shapes=[pltpu.VMEM((d, tf), BF16), pltpu.VMEM((d, tf), BF16)]),
        out_shape=jax.ShapeDtypeStruct((n_rows, f), BF16),
        compiler_params=_params("arbitrary", "arbitrary"),
        name="moe_gate",
    )(*tables, xg, w1, w3)
    return pl.pallas_call(
        _moe_down_kernel,
        grid_spec=pltpu.PrefetchScalarGridSpec(
            num_scalar_prefetch=3,
            grid=(d // tn, n_items),
            in_specs=[pl.BlockSpec((sub, f), lambda j, i, exp, first, valid: (i, 0)),
                      pl.BlockSpec((None, None, f, tn), lambda j, i, exp, first, valid: (moe_idx, exp[i], 0, j))],
            out_specs=pl.BlockSpec((sub, tn), lambda j, i, exp, first, valid: (i, j)),
            scratch_shapes=[pltpu.VMEM((f, tn), BF16)]),
        out_shape=jax.ShapeDtypeStruct((n_rows, d), F32),
        compiler_params=_params("arbitrary", "arbitrary"),
        name="moe_down",
    )(*tables, g, w2)


def _combine_kernel(pos_ref, pos_next_ref, y_hbm, route_ref, x_ref, postg_ref, mod_ref, xo_ref, buf, sem, *, tt):
    i = pl.program_id(0)
    slot = i % 2

    def row_copy(idx_ref, s, k, r):
        return pltpu.make_async_copy(y_hbm.at[pl.ds(idx_ref[0, 0, TOP_K * r + k], 1)],
                                     buf.at[s, k, pl.ds(r, 1)], sem.at[s])

    def start_all(idx_ref, s):
        def start(r, carry):
            for k in range(TOP_K):
                row_copy(idx_ref, s, k, r).start()
            return carry
        lax.fori_loop(0, tt, start, 0, unroll=4)

    @pl.when(i == 0)
    def _():
        start_all(pos_ref, 0)

    @pl.when(i + 1 < pl.num_programs(0))
    def _():
        start_all(pos_next_ref, 1 - slot)

    def wait(r, carry):
        for k in range(TOP_K):
            row_copy(pos_ref, slot, k, r).wait()
        return carry

    lax.fori_loop(0, tt, wait, 0, unroll=4)
    route = route_ref[...]
    f = route[:, 2:3] * buf[slot, 0] + route[:, 3:4] * buf[slot, 1]
    xo_ref[0] = _residual(f, x_ref[0], postg_ref[...], mod_ref, G2)


def _combine(y, pos, route, x, post_g, mod, tt=256):
    bsz, t, d = x.shape
    n = bsz * t
    tpb = t // tt
    n_tiles = n // tt
    pos3 = pos.reshape(n_tiles, 1, TOP_K * tt)
    return pl.pallas_call(
        functools.partial(_combine_kernel, tt=tt),
        grid=(n_tiles,),
        in_specs=[pl.BlockSpec((1, 1, TOP_K * tt), lambda i: (i, 0, 0), memory_space=pltpu.SMEM),
                  pl.BlockSpec((1, 1, TOP_K * tt), lambda i: (jnp.minimum(i + 1, n_tiles - 1), 0, 0),
                               memory_space=pltpu.SMEM),
                  pl.BlockSpec(memory_space=pl.ANY),
                  pl.BlockSpec((tt, LANES), lambda i: (i, 0)),
                  pl.BlockSpec((1, tt, d), lambda i: (i // tpb, i % tpb, 0)),
                  pl.BlockSpec((1, d), lambda i: (0, 0)),
                  pl.BlockSpec((1, 6, d), lambda i: (i // tpb, 0, 0))],
        out_specs=pl.BlockSpec((1, tt, d), lambda i: (i // tpb, i % tpb, 0)),
        out_shape=jax.ShapeDtypeStruct((bsz, t, d), F32),
        scratch_shapes=[pltpu.VMEM((2, TOP_K, tt, d), F32), pltpu.SemaphoreType.DMA((2,))],
        compiler_params=_params("arbitrary"),
        name="moe_combine",
    )(pos3, pos3, y, route, x, post_g.reshape(1, d), mod)


def _routing_tables(route, n_experts, sub, n_items):
    experts = route[:, :TOP_K].astype(jnp.int32).reshape(-1)
    onehot = (experts[:, None] == jnp.arange(n_experts, dtype=jnp.int32)[None, :]).astype(jnp.int32)
    csum = jnp.cumsum(onehot, axis=0)
    rank = jnp.sum(csum * onehot, axis=1) - 1
    counts = csum[-1]
    blocks = (counts + sub - 1) // sub
    block_end = jnp.cumsum(blocks)
    offs = (block_end - blocks) * sub
    pos = offs[experts] + rank
    n_valid = block_end[-1]
    item = jnp.arange(n_items, dtype=jnp.int32)
    blk = jnp.minimum(item, n_valid - 1)
    exp = jnp.minimum(jnp.sum((blk[:, None] >= block_end[None, :]).astype(jnp.int32), axis=1), n_experts - 1)
    valid = (item < n_valid).astype(jnp.int32)
    prev = jnp.concatenate([jnp.full((1,), -1, jnp.int32), exp[:-1]])
    first = ((exp != prev) & (valid == 1)).astype(jnp.int32)
    return pos.astype(jnp.int32), (exp, first, valid)


def _moe(h, x, router_w, w1, w3, w2, moe_idx, post_g, mod, sub=512):
    bsz, t, d = x.shape
    n = bsz * t
    n_experts = router_w.shape[-1]
    n_items = (n * TOP_K) // sub + n_experts
    hf = h.reshape(n, d)
    route = _router(hf, router_w[moe_idx])
    pos, tables = _routing_tables(route, n_experts, sub, n_items)
    xg = _dispatch(hf, _invert(pos, n_items * sub), sub)
    y = _moe_experts(xg, tables, w1, w3, w2, moe_idx, sub)
    return _combine(y, pos, route, x, post_g, mod)


def kernel(x, c, ctx, c_ctx, w_ada, b_ada, pre_mix_g, post_mix_g, pre_ffn_g, post_ffn_g, w_in, gate_w2_f, gate_b_f, gate_w2_b, gate_b_b, gla_norm_g, conv_w, conv_b, w_proj_a, w_proj_b, w_out, ffn_w1, ffn_w3, ffn_w2, router_w, moe_w1, moe_w3, moe_w2):
    bsz, seq, d = x.shape
    ctx_len = ctx.shape[1]
    depth = w_ada.shape[0]
    dk = gate_w2_f.shape[-1]
    dv = gla_norm_g.shape[-1]
    hk, hv = dk // N_HEADS, dv // N_HEADS
    off_lr = 2 * dk + dv
    n_lr = 2 * GATE_RANK
    n_qkv = 2 * dk + dv
    off_r = 0
    off_cb = off_r + dv
    off_ga = off_cb + 3 * d
    off_gb = off_ga + d
    n_gates = off_gb + d

    cc = jnp.concatenate([c, c_ctx[None, :], jnp.zeros((8 - bsz - 1, d), F32)], axis=0)
    mod_all = _ada(cc, w_ada, b_ada)

    xc = ctx
    hx = hc = None
    for layer in range(depth):
        last = layer == depth - 1
        mod_x = mod_all[layer, :bsz].reshape(bsz, 6, d)
        mod_c = jnp.broadcast_to(mod_all[layer, bsz].reshape(1, 6, d), (bsz, 6, d))

        w_qkv = w_in[layer, :, :n_qkv].astype(BF16)
        w_gates = w_in[layer, :, off_lr + n_lr:].astype(BF16)
        w_lr = jnp.zeros((d, LANES), BF16).at[:, :n_lr].set(w_in[layer, :, off_lr:off_lr + n_lr].astype(BF16))
        w2 = jnp.zeros((LANES, 2 * dk), BF16)
        w2 = w2.at[:GATE_RANK, :dk].set(gate_w2_f[layer].astype(BF16))
        w2 = w2.at[GATE_RANK:n_lr, dk:].set(gate_w2_b[layer].astype(BF16))
        gate_b = jnp.concatenate([gate_b_f[layer], gate_b_b[layer]]).reshape(1, 2 * dk)
        w_a = w_proj_a[layer].astype(BF16)
        w_b = w_proj_b[layer].astype(BF16)
        w_o = w_out[layer].astype(BF16)

        if hx is None:
            hx = _prenorm(x, pre_mix_g[layer], mod_x)
            hc = _prenorm(xc, pre_mix_g[layer], mod_c)

        hc2 = hc.reshape(bsz * ctx_len, d)
        pc = _matmul(hc2, w_qkv, n_qkv).reshape(bsz, ctx_len, n_qkv)
        la_c = _decay(hc2, w_lr, w2, gate_b).reshape(bsz, ctx_len, 2 * dk)
        s0 = jnp.zeros((bsz, N_HEADS, hv, hk), F32)
        if last:
            s_f, s_b = _gla(pc, None, la_c, s0, s0, None, dk, dv, off_r, want_out=False)
        else:
            gc = _matmul(hc2, w_gates, n_gates).reshape(bsz, ctx_len, n_gates)
            o_c, s_f, s_b = _gla(pc, gc, la_c, s0, s0, gla_norm_g[layer], dk, dv, off_r, want_out=True)

        hx2 = hx.reshape(bsz * seq, d)
        px = _matmul(hx2, w_qkv, n_qkv).reshape(bsz, seq, n_qkv)
        gx = _matmul(hx2, w_gates, n_gates).reshape(bsz, seq, n_gates)
        la_x = _decay(hx2, w_lr, w2, gate_b).reshape(bsz, seq, 2 * dk)
        o_x, _, _ = _gla(px, gx, la_x, s_f, s_b, gla_norm_g[layer], dk, dv, off_r, want_out=True)
        a_x = _gated_conv(gx, conv_w[layer], conv_b[layer], d, off_cb, seg=GRID_W)
        merged = _merge(a_x.reshape(bsz * seq, d), o_x.reshape(bsz * seq, d), w_a, w_b,
                        gx.reshape(bsz * seq, n_gates), off_ga, off_gb)
        moe_layer = layer % 2 == 1
        x, h2 = _outproj(merged, w_o, x, post_mix_g[layer], pre_ffn_g[layer], mod_x,
                         F32 if moe_layer else BF16)
        nxt = None if last else (pre_mix_g[layer + 1], mod_all[layer + 1, :bsz].reshape(bsz, 6, d))
        if moe_layer:
            x = _moe(h2, x, router_w, moe_w1, moe_w3, moe_w2, layer // 2, post_ffn_g[layer], mod_x)
            if nxt is not None:
                hx = _prenorm(x, nxt[0], nxt[1])
        else:
            f_x = _ffn(h2.reshape(bsz * seq, d), ffn_w1, ffn_w3, ffn_w2, layer // 2)
            if nxt is None:
                (x,) = _ffn_residual(f_x, x, post_ffn_g[layer], mod_x)
            else:
                x, hx = _ffn_residual(f_x, x, post_ffn_g[layer], mod_x, nxt[0], nxt[1])

        if not last:
            a_c = _gated_conv(gc, conv_w[layer], conv_b[layer], d, off_cb, seg=ctx_len, tt=ctx_len)
            merged_c = _merge(a_c.reshape(bsz * ctx_len, d), o_c.reshape(bsz * ctx_len, d), w_a, w_b,
                              gc.reshape(bsz * ctx_len, n_gates), off_ga, off_gb)
            xc, h2c = _outproj(merged_c, w_o, xc, post_mix_g[layer], pre_ffn_g[layer], mod_c,
                               F32 if moe_layer else BF16)
            mod_cn = jnp.broadcast_to(mod_all[layer + 1, bsz].reshape(1, 6, d), (bsz, 6, d))
            if moe_layer:
                xc = _moe(h2c, xc, router_w, moe_w1, moe_w3, moe_w2, layer // 2, post_ffn_g[layer], mod_c)
                hc = _prenorm(xc, pre_mix_g[layer + 1], mod_cn)
            else:
                f_c = _ffn(h2c.reshape(bsz * ctx_len, d), ffn_w1, ffn_w3, ffn_w2, layer // 2)
                xc, hc = _ffn_residual(f_c, xc, post_ffn_g[layer], mod_c, pre_mix_g[layer + 1], mod_cn)
    return x
```

```python
import functools

import jax
import jax.numpy as jnp
from jax import lax
from jax.experimental import pallas as pl
from jax.experimental.pallas import tpu as pltpu

F32 = jnp.float32
BF16 = jnp.bfloat16

EPS = 1e-6
N_HEADS = 4
GATE_RANK = 16
GATE_TAU = 16.0
GLA_CHUNK = 64
GRID_W = 64
TOP_K = 2

LANES = 128
VMEM_LIMIT = 56 * 1024 * 1024

SH1, SC1, G1, SH2, SC2, G2 = range(6)


def _params(*sem):
    return pltpu.CompilerParams(dimension_semantics=sem, vmem_limit_bytes=VMEM_LIMIT)


def _dot(a, b):
    return jnp.dot(a, b, preferred_element_type=F32)


def _dot_nt(a, b):
    return lax.dot_general(a, b, (((1,), (1,)), ((), ())), preferred_element_type=F32)


def _dot_tn(a, b):
    return lax.dot_general(a, b, (((0,), (0,)), ((), ())), preferred_element_type=F32)


def _silu(x):
    return x * jax.nn.sigmoid(x)


def _rms(x):
    return x * lax.rsqrt(jnp.mean(x * x, axis=-1, keepdims=True) + EPS)


def _ada_kernel(c_ref, w_ref, b_ref, o_ref):
    s = _silu(c_ref[...]).astype(BF16)
    o_ref[0] = _dot(s, w_ref[0].astype(BF16)) + b_ref[0]


def _ada(cc, w_ada, b_ada, tn=1024):
    depth, d, n = w_ada.shape
    rows = cc.shape[0]
    return pl.pallas_call(
        _ada_kernel,
        grid=(depth, n // tn),
        in_specs=[pl.BlockSpec((rows, d), lambda l, j: (0, 0)),
                  pl.BlockSpec((1, d, tn), lambda l, j: (l, 0, j)),
                  pl.BlockSpec((1, 1, tn), lambda l, j: (l, 0, j))],
        out_specs=pl.BlockSpec((1, rows, tn), lambda l, j: (l, 0, j)),
        out_shape=jax.ShapeDtypeStruct((depth, rows, n), F32),
        compiler_params=_params("parallel", "parallel"),
        name="ada",
    )(cc, w_ada, b_ada.reshape(depth, 1, n))


def _modulated_norm(x, g, mod_ref, shift_row, scale_row):
    h = _rms(x) * g
    return h * (1.0 + mod_ref[0, scale_row:scale_row + 1, :]) + mod_ref[0, shift_row:shift_row + 1, :]


def _prenorm_kernel(x_ref, g_ref, mod_ref, o_ref):
    o_ref[0] = _modulated_norm(x_ref[0], g_ref[...], mod_ref, SH1, SC1).astype(o_ref.dtype)


def _prenorm(x, g, mod, tt=256):
    b, t, d = x.shape
    return pl.pallas_call(
        _prenorm_kernel,
        grid=(b, t // tt),
        in_specs=[pl.BlockSpec((1, tt, d), lambda i, j: (i, j, 0)),
                  pl.BlockSpec((1, d), lambda i, j: (0, 0)),
                  pl.BlockSpec((1, 6, d), lambda i, j: (i, 0, 0))],
        out_specs=pl.BlockSpec((1, tt, d), lambda i, j: (i, j, 0)),
        out_shape=jax.ShapeDtypeStruct((b, t, d), BF16),
        compiler_params=_params("parallel", "parallel"),
        name="prenorm",
    )(x, g.reshape(1, d), mod)


def _mm_kernel(h_ref, w_ref, o_ref):
    o_ref[...] = _dot(h_ref[...], w_ref[...])


def _matmul(h, w, ncols, tm=1024, tn=1024):
    n, k = h.shape
    tm = min(tm, n)
    return pl.pallas_call(
        _mm_kernel,
        grid=(n // tm, ncols // tn),
        in_specs=[pl.BlockSpec((tm, k), lambda i, j: (i, 0)),
                  pl.BlockSpec((k, tn), lambda i, j: (0, j))],
        out_specs=pl.BlockSpec((tm, tn), lambda i, j: (i, j)),
        out_shape=jax.ShapeDtypeStruct((n, ncols), F32),
        compiler_params=_params("parallel", "parallel"),
        name="in_proj",
    )(h, w)


def _decay_kernel(h_ref, wlr_ref, w2_ref, b_ref, o_ref):
    lr = _dot(h_ref[...], wlr_ref[...]).astype(BF16)
    z = _dot(lr, w2_ref[...]) + b_ref[...]
    log_sig = jnp.minimum(z, 0.0) - jnp.log1p(jnp.exp(-jnp.abs(z)))
    o_ref[...] = log_sig / GATE_TAU


def _decay(h, w_lr, w2, bias, tm=512):
    n, k = h.shape
    r = w_lr.shape[1]
    m = w2.shape[1]
    return pl.pallas_call(
        _decay_kernel,
        grid=(n // tm,),
        in_specs=[pl.BlockSpec((tm, k), lambda i: (i, 0)),
                  pl.BlockSpec((k, r), lambda i: (0, 0)),
                  pl.BlockSpec((r, m), lambda i: (0, 0)),
                  pl.BlockSpec((1, m), lambda i: (0, 0))],
        out_specs=pl.BlockSpec((tm, m), lambda i: (i, 0)),
        out_shape=jax.ShapeDtypeStruct((n, m), F32),
        compiler_params=_params("parallel"),
        name="decay",
    )(h, w_lr, w2, bias)


def _gla_kernel(*refs, t, scale, want_out):
    if want_out:
        (q_ref, k_ref, v_ref, laf_ref, lab_ref, s0f_ref, s0b_ref, r_ref, g_ref,
         o_ref, sf_ref, sb_ref,
         of_scr, ob_scr, qtf_scr, qtb_scr, kef_scr, keb_scr, df_scr, db_scr, stf_scr, stb_scr) = refs
    else:
        (q_ref, k_ref, v_ref, laf_ref, lab_ref, s0f_ref, s0b_ref,
         sf_ref, sb_ref, kef_scr, keb_scr, df_scr, db_scr, stf_scr, stb_scr) = refs
        of_scr = ob_scr = qtf_scr = qtb_scr = None
    c_len = GLA_CHUNK
    n_chunks = t // c_len
    blk_len = min(4 * c_len, t)
    n_blocks = t // blk_len
    per_blk = blk_len // c_len
    hk = k_ref.shape[-1]
    row = lax.broadcasted_iota(jnp.int32, (blk_len, blk_len), 0)
    col = lax.broadcasted_iota(jnp.int32, (blk_len, blk_len), 1)
    same_chunk = (row // c_len) == (col // c_len)
    mask_f = same_chunk & (row >= col)
    mask_b = same_chunk & (row <= col)
    tri_f = mask_f.astype(BF16)
    tri_b = mask_b.astype(BF16)

    dirs = ((laf_ref, tri_f, mask_f, c_len - 1, qtf_scr, kef_scr, df_scr, of_scr),
            (lab_ref, tri_b, mask_b, 0, qtb_scr, keb_scr, db_scr, ob_scr))

    def prepare_body(blk, carry):
        sl = pl.ds(pl.multiple_of(blk * blk_len, blk_len), blk_len)
        k = k_ref[0, sl, :]
        cum = []
        for la_ref, tri, *_ in dirs:
            la = la_ref[0, sl, :]
            la_hi = la.astype(BF16)
            la_lo = (la - la_hi.astype(F32)).astype(BF16)
            cum.append(_dot(tri, la_hi) + _dot(tri, la_lo))
        decayed = []
        for b, (_, _, _, last_row, qt_scr, ke_scr, d_scr, _) in zip(cum, dirs):
            b_last = jnp.concatenate(
                [jnp.broadcast_to(b[i * c_len + last_row:i * c_len + last_row + 1, :], (c_len, hk))
                 for i in range(per_blk)], axis=0)
            ke_scr[sl, :] = (k * jnp.exp(b_last - b)).astype(BF16)
            d = jnp.exp(b_last)
            for i in range(per_blk):
                d_scr[pl.ds(pl.multiple_of((blk * per_blk + i) * 8, 8), 8), :] = d[i * c_len:i * c_len + 8, :]
            if want_out:
                qt = (q_ref[0, sl, :] * scale * jnp.exp(b)).astype(BF16)
                qt_scr[sl, :] = qt
                decayed.append((qt, (k * jnp.exp(-b)).astype(BF16)))
        if want_out:
            scores = [_dot_nt(qt, kt) for qt, kt in decayed]
            v = v_ref[0, sl, :].astype(BF16)
            for sc, (_, _, mask, _, _, _, _, o_scr) in zip(scores, dirs):
                o_scr[sl, :] = _dot(jnp.where(mask, sc, 0.0).astype(BF16), v)
        return carry

    lax.fori_loop(0, n_blocks, prepare_body, 0)

    stf_scr[...] = s0f_ref[0, 0]
    stb_scr[...] = s0b_ref[0, 0]

    def scan_step(c, qt_scr, ke_scr, d_scr, o_scr, st_scr):
        sl = pl.ds(pl.multiple_of(c * c_len, c_len), c_len)
        st = st_scr[...]
        if want_out:
            o_scr[sl, :] += _dot_nt(qt_scr[sl, :], st.astype(BF16))
        d = d_scr[pl.ds(pl.multiple_of(c * 8, 8), 8), :]
        st_scr[...] = d[0:1, :] * st + _dot_tn(v_ref[0, sl, :].astype(BF16), ke_scr[sl, :])

    def scan_body(c, carry):
        scan_step(c, qtf_scr, kef_scr, df_scr, of_scr, stf_scr)
        scan_step(n_chunks - 1 - c, qtb_scr, keb_scr, db_scr, ob_scr, stb_scr)
        return carry

    lax.fori_loop(0, n_chunks, scan_body, 0, unroll=2)
    sf_ref[0, 0] = stf_scr[...]
    sb_ref[0, 0] = stb_scr[...]

    if want_out:
        def epilogue(c, carry):
            sl = pl.ds(pl.multiple_of(c * c_len, c_len), c_len)
            o = _rms(of_scr[sl, :] + ob_scr[sl, :]) * g_ref[...]
            o_ref[0, sl, :] = (o * _silu(r_ref[0, sl, :])).astype(o_ref.dtype)
            return carry

        lax.fori_loop(0, n_chunks, epilogue, 0)


def _gla(p, p_r, la, s0f, s0b, gla_g, dk_total, dv_total, off_r, want_out):
    bsz, t, _ = p.shape
    hk = dk_total // N_HEADS
    hv = dv_total // N_HEADS
    nkb = dk_total // hk
    kern = functools.partial(_gla_kernel, t=t, scale=hk ** -0.5, want_out=want_out)
    n_chunks = t // GLA_CHUNK
    in_specs = [
        pl.BlockSpec((1, t, hk), lambda b, h: (b, 0, h)),
        pl.BlockSpec((1, t, hk), lambda b, h: (b, 0, nkb + h)),
        pl.BlockSpec((1, t, hv), lambda b, h: (b, 0, 2 * dk_total // hv + h)),
        pl.BlockSpec((1, t, hk), lambda b, h: (b, 0, h)),
        pl.BlockSpec((1, t, hk), lambda b, h: (b, 0, nkb + h)),
        pl.BlockSpec((1, 1, hv, hk), lambda b, h: (b, h, 0, 0)),
        pl.BlockSpec((1, 1, hv, hk), lambda b, h: (b, h, 0, 0)),
    ]
    args = [p, p, p, la, la, s0f, s0b]
    st_shape = jax.ShapeDtypeStruct((bsz, N_HEADS, hv, hk), F32)
    st_spec = pl.BlockSpec((1, 1, hv, hk), lambda b, h: (b, h, 0, 0))
    scratch = [pltpu.VMEM((t, hk), BF16), pltpu.VMEM((t, hk), BF16),
               pltpu.VMEM((n_chunks * 8, hk), F32), pltpu.VMEM((n_chunks * 8, hk), F32),
               pltpu.VMEM((hv, hk), F32), pltpu.VMEM((hv, hk), F32)]
    if want_out:
        in_specs += [pl.BlockSpec((1, t, hv), lambda b, h: (b, 0, off_r // hv + h)),
                     pl.BlockSpec((1, hv), lambda b, h: (0, h))]
        args += [p_r, gla_g.reshape(1, dv_total)]
        out_shape = (jax.ShapeDtypeStruct((bsz, t, dv_total), BF16), st_shape, st_shape)
        out_specs = (pl.BlockSpec((1, t, hv), lambda b, h: (b, 0, h)), st_spec, st_spec)
        scratch = [pltpu.VMEM((t, hv), F32), pltpu.VMEM((t, hv), F32),
                   pltpu.VMEM((t, hk), BF16), pltpu.VMEM((t, hk), BF16)] + scratch
    else:
        out_shape = (st_shape, st_shape)
        out_specs = (st_spec, st_spec)
    return pl.pallas_call(
        kern,
        grid=(bsz, N_HEADS),
        in_specs=in_specs,
        out_specs=out_specs,
        out_shape=out_shape,
        scratch_shapes=scratch,
        compiler_params=_params("parallel", "parallel"),
        name="gla",
    )(*args)


def _conv_kernel(cb_ref, cc_ref, cx_ref, w_ref, b_ref, o_ref, *, seg):
    u = cc_ref[0] * cx_ref[0]
    tt = u.shape[0]
    pos = lax.broadcasted_iota(jnp.int32, u.shape, 0) % seg
    u_prev = jnp.where(pos == 0, 0.0, pltpu.roll(u, 1, axis=0))
    u_next = jnp.where(pos == seg - 1, 0.0, pltpu.roll(u, tt - 1, axis=0))
    conv = b_ref[...] + u_prev * w_ref[0:1, :]
    conv = conv + u * w_ref[1:2, :]
    conv = conv + u_next * w_ref[2:3, :]
    o_ref[0] = (cb_ref[0] * conv).astype(o_ref.dtype)


def _gated_conv(p, conv_w, conv_b, d, off_cb, seg, tt=256):
    bsz, t, _ = p.shape
    cb = off_cb // d
    kern = functools.partial(_conv_kernel, seg=seg)
    return pl.pallas_call(
        kern,
        grid=(bsz, t // tt),
        in_specs=[pl.BlockSpec((1, tt, d), lambda b, i: (b, i, cb)),
                  pl.BlockSpec((1, tt, d), lambda b, i: (b, i, cb + 1)),
                  pl.BlockSpec((1, tt, d), lambda b, i: (b, i, cb + 2)),
                  pl.BlockSpec((3, d), lambda b, i: (0, 0)),
                  pl.BlockSpec((1, d), lambda b, i: (0, 0))],
        out_specs=pl.BlockSpec((1, tt, d), lambda b, i: (b, i, 0)),
        out_shape=jax.ShapeDtypeStruct((bsz, t, d), BF16),
        compiler_params=_params("parallel", "parallel"),
        name="gated_conv",
    )(p, p, p, conv_w, conv_b.reshape(1, d))


def _merge_kernel(a_ref, b_ref, wa_ref, wb_ref, ga_ref, gb_ref, o_ref):
    y_a = _dot(a_ref[...], wa_ref[...])
    y_b = _dot(b_ref[...], wb_ref[...])
    o_ref[...] = (jax.nn.sigmoid(ga_ref[...]) * y_a + jax.nn.sigmoid(gb_ref[...]) * y_b).astype(o_ref.dtype)


def _merge(a_in, b_in, w_a, w_b, p, off_ga, off_gb, tm=1024, tn=512):
    n, d = a_in.shape
    tm = min(tm, n)
    return pl.pallas_call(
        _merge_kernel,
        grid=(n // tm, d // tn),
        in_specs=[pl.BlockSpec((tm, d), lambda i, j: (i, 0)),
                  pl.BlockSpec((tm, d), lambda i, j: (i, 0)),
                  pl.BlockSpec((d, tn), lambda i, j: (0, j)),
                  pl.BlockSpec((d, tn), lambda i, j: (0, j)),
                  pl.BlockSpec((tm, tn), lambda i, j: (i, off_ga // tn + j)),
                  pl.BlockSpec((tm, tn), lambda i, j: (i, off_gb // tn + j))],
        out_specs=pl.BlockSpec((tm, tn), lambda i, j: (i, j)),
        out_shape=jax.ShapeDtypeStruct((n, d), BF16),
        compiler_params=_params("parallel", "parallel"),
        name="merge",
    )(a_in, b_in, w_a, w_b, p, p)


def _residual(y, x, post_g, mod_ref, gate_row):
    return x + mod_ref[0, gate_row:gate_row + 1, :] * (_rms(y) * post_g)


def _outproj_kernel(m_ref, w_ref, x_ref, postg_ref, preg_ref, mod_ref, xo_ref, h_ref):
    y = _dot(m_ref[0], w_ref[...])
    x_new = _residual(y, x_ref[0], postg_ref[...], mod_ref, G1)
    xo_ref[0] = x_new
    h_ref[0] = _modulated_norm(x_new, preg_ref[...], mod_ref, SH2, SC2).astype(h_ref.dtype)


def _outproj(merged, w_o, x, post_g, pre_g, mod, h_dtype, tm=512):
    bsz, t, d = x.shape
    tm = min(tm, t)
    return pl.pallas_call(
        _outproj_kernel,
        grid=(bsz, t // tm),
        in_specs=[pl.BlockSpec((1, tm, d), lambda b, i: (b, i, 0)),
                  pl.BlockSpec((d, d), lambda b, i: (0, 0)),
                  pl.BlockSpec((1, tm, d), lambda b, i: (b, i, 0)),
                  pl.BlockSpec((1, d), lambda b, i: (0, 0)),
                  pl.BlockSpec((1, d), lambda b, i: (0, 0)),
                  pl.BlockSpec((1, 6, d), lambda b, i: (b, 0, 0))],
        out_specs=(pl.BlockSpec((1, tm, d), lambda b, i: (b, i, 0)),
                   pl.BlockSpec((1, tm, d), lambda b, i: (b, i, 0))),
        out_shape=(jax.ShapeDtypeStruct((bsz, t, d), F32),
                   jax.ShapeDtypeStruct((bsz, t, d), h_dtype)),
        compiler_params=_params("parallel", "parallel"),
        name="out_proj",
    )(merged.reshape(bsz, t, d), w_o, x, post_g.reshape(1, d), pre_g.reshape(1, d), mod)


def _ffn_res_kernel(*refs, with_next):
    if with_next:
        f_ref, x_ref, postg_ref, mod_ref, preg_ref, modn_ref, xo_ref, h_ref = refs
    else:
        f_ref, x_ref, postg_ref, mod_ref, xo_ref = refs
    x_new = _residual(f_ref[0], x_ref[0], postg_ref[...], mod_ref, G2)
    xo_ref[0] = x_new
    if with_next:
        h_ref[0] = _modulated_norm(x_new, preg_ref[...], modn_ref, SH1, SC1).astype(h_ref.dtype)


def _ffn_residual(f, x, post_g, mod, next_pre_g=None, next_mod=None, tt=256):
    bsz, t, d = x.shape
    with_next = next_pre_g is not None
    tile = pl.BlockSpec((1, tt, d), lambda b, i: (b, i, 0))
    vec = pl.BlockSpec((1, d), lambda b, i: (0, 0))
    modspec = pl.BlockSpec((1, 6, d), lambda b, i: (b, 0, 0))
    in_specs = [tile, tile, vec, modspec]
    args = [f.reshape(bsz, t, d), x, post_g.reshape(1, d), mod]
    out_shape = [jax.ShapeDtypeStruct((bsz, t, d), F32)]
    out_specs = [tile]
    if with_next:
        in_specs += [vec, modspec]
        args += [next_pre_g.reshape(1, d), next_mod]
        out_shape.append(jax.ShapeDtypeStruct((bsz, t, d), BF16))
        out_specs.append(tile)
    return pl.pallas_call(
        functools.partial(_ffn_res_kernel, with_next=with_next),
        grid=(bsz, t // tt),
        in_specs=in_specs,
        out_specs=tuple(out_specs),
        out_shape=tuple(out_shape),
        compiler_params=_params("parallel", "parallel"),
        name="ffn_residual",
    )(*args)


def _ffn_gate_kernel(h_ref, w1_ref, w3_ref, g_ref):
    h = h_ref[...]
    a = _dot(h, w1_ref[...].astype(BF16))
    b = _dot(h, w3_ref[...].astype(BF16))
    g_ref[...] = (_silu(a) * b).astype(g_ref.dtype)


def _ffn_down_kernel(g_ref, w2_ref, o_ref):
    o_ref[...] = _dot(g_ref[...], w2_ref[...].astype(BF16))


def _ffn(h, w1, w3, w2, layer_idx, tm=1024, tf=512, tn=256):
    n, d = h.shape
    f = w1.shape[-1]
    tm = min(tm, n)
    g = pl.pallas_call(
        _ffn_gate_kernel,
        grid=(n // tm, f // tf),
        in_specs=[pl.BlockSpec((tm, d), lambda i, j: (i, 0)),
                  pl.BlockSpec((None, d, tf), lambda i, j: (layer_idx, 0, j)),
                  pl.BlockSpec((None, d, tf), lambda i, j: (layer_idx, 0, j))],
        out_specs=pl.BlockSpec((tm, tf), lambda i, j: (i, j)),
        out_shape=jax.ShapeDtypeStruct((n, f), BF16),
        compiler_params=_params("parallel", "parallel"),
        name="ffn_gate",
    )(h, w1, w3)
    return pl.pallas_call(
        _ffn_down_kernel,
        grid=(n // tm, d // tn),
        in_specs=[pl.BlockSpec((tm, f), lambda i, j: (i, 0)),
                  pl.BlockSpec((None, f, tn), lambda i, j: (layer_idx, 0, j))],
        out_specs=pl.BlockSpec((tm, tn), lambda i, j: (i, j)),
        out_shape=jax.ShapeDtypeStruct((n, d), F32),
        compiler_params=_params("parallel", "parallel"),
        name="ffn_down",
    )(g, w2)


def _router_kernel(h_ref, w_ref, o_ref, *, n_experts):
    h = h_ref[...]
    w = w_ref[...]
    h_hi = h.astype(BF16)
    h_lo = (h - h_hi.astype(F32)).astype(BF16)
    w_hi = w.astype(BF16)
    w_lo = (w - w_hi.astype(F32)).astype(BF16)
    logits = _dot(h_hi, w_hi) + (_dot(h_hi, w_lo) + _dot(h_lo, w_hi))
    lane = lax.broadcasted_iota(jnp.int32, logits.shape, 1)
    neg = -jnp.inf
    l1 = jnp.where(lane < n_experts, logits, neg)
    m1 = jnp.max(l1, axis=-1, keepdims=True)
    i1 = jnp.min(jnp.where(l1 == m1, lane, LANES), axis=-1, keepdims=True)
    l2 = jnp.where(lane == i1, neg, l1)
    m2 = jnp.max(l2, axis=-1, keepdims=True)
    i2 = jnp.min(jnp.where(l2 == m2, lane, LANES), axis=-1, keepdims=True)
    e = jnp.exp(m2 - m1)
    p1 = 1.0 / (1.0 + e)
    p2 = e / (1.0 + e)
    out = jnp.where(lane == 0, i1.astype(F32),
                    jnp.where(lane == 1, i2.astype(F32),
                              jnp.where(lane == 2, p1, jnp.where(lane == 3, p2, 0.0))))
    o_ref[...] = out


def _router(h, w_router, tm=512):
    n, d = h.shape
    n_experts = w_router.shape[1]
    w_pad = jnp.zeros((d, LANES), F32).at[:, :n_experts].set(w_router)
    return pl.pallas_call(
        functools.partial(_router_kernel, n_experts=n_experts),
        grid=(n // tm,),
        in_specs=[pl.BlockSpec((tm, d), lambda i: (i, 0)),
                  pl.BlockSpec((d, LANES), lambda i: (0, 0))],
        out_specs=pl.BlockSpec((tm, LANES), lambda i: (i, 0)),
        out_shape=jax.ShapeDtypeStruct((n, LANES), F32),
        compiler_params=_params("parallel"),
        name="router",
    )(h, w_pad)


def _invert_kernel(pos_ref, tok_ref, *, n_pairs, n_rows):
    def zero(r, carry):
        tok_ref[r] = 0
        return carry

    def scatter(p, carry):
        tok_ref[pos_ref[p]] = p // TOP_K
        return carry

    lax.fori_loop(0, n_rows, zero, 0, unroll=16)
    lax.fori_loop(0, n_pairs, scatter, 0, unroll=16)


def _invert(pos, n_rows):
    return pl.pallas_call(
        functools.partial(_invert_kernel, n_pairs=pos.shape[0], n_rows=n_rows),
        in_specs=[pl.BlockSpec(memory_space=pltpu.SMEM)],
        out_specs=pl.BlockSpec(memory_space=pltpu.SMEM),
        out_shape=jax.ShapeDtypeStruct((n_rows,), jnp.int32),
        name="moe_invert",
    )(pos)


def _dispatch_kernel(tok_ref, tok_next_ref, h_hbm, xg_ref, buf, sem, *, sub):
    i = pl.program_id(0)
    slot = i % 2

    def row_copy(idx_ref, s, r):
        return pltpu.make_async_copy(h_hbm.at[pl.ds(idx_ref[0, 0, r], 1)], buf.at[s, pl.ds(r, 1)], sem.at[s])

    def start_all(idx_ref, s):
        def start(r, carry):
            row_copy(idx_ref, s, r).start()
            return carry
        lax.fori_loop(0, sub, start, 0, unroll=8)

    @pl.when(i == 0)
    def _():
        start_all(tok_ref, 0)

    @pl.when(i + 1 < pl.num_programs(0))
    def _():
        start_all(tok_next_ref, 1 - slot)

    def wait(r, carry):
        row_copy(tok_ref, slot, r).wait()
        return carry

    lax.fori_loop(0, sub, wait, 0, unroll=8)
    xg_ref[...] = buf[slot].astype(xg_ref.dtype)


def _dispatch(h, row_tok, sub):
    n, d = h.shape
    n_rows = row_tok.shape[0]
    n_blocks = n_rows // sub
    tok3 = row_tok.reshape(n_blocks, 1, sub)
    return pl.pallas_call(
        functools.partial(_dispatch_kernel, sub=sub),
        grid=(n_blocks,),
        in_specs=[pl.BlockSpec((1, 1, sub), lambda i: (i, 0, 0), memory_space=pltpu.SMEM),
                  pl.BlockSpec((1, 1, sub), lambda i: (jnp.minimum(i + 1, n_blocks - 1), 0, 0),
                               memory_space=pltpu.SMEM),
                  pl.BlockSpec(memory_space=pl.ANY)],
        out_specs=pl.BlockSpec((sub, d), lambda i: (i, 0)),
        out_shape=jax.ShapeDtypeStruct((n_rows, d), BF16),
        scratch_shapes=[pltpu.VMEM((2, sub, d), F32), pltpu.SemaphoreType.DMA((2,))],
        compiler_params=_params("arbitrary"),
        name="moe_dispatch",
    )(tok3, tok3, h)


def _grouped_kernel(start_ref, count_ref, x_hbm, *rest, n_w, compute, sub, tn, n_blocks):
    w_refs = rest[:n_w]
    y_hbm = rest[n_w]
    xbuf, ybuf = rest[n_w + 1:n_w + 3]
    wb_refs = rest[n_w + 3:2 * n_w + 3]
    in_sem, out_sem = rest[2 * n_w + 3:]
    j = pl.program_id(0)
    e = pl.program_id(1)
    n = count_ref[e]
    base = start_ref[e]
    col = pl.multiple_of(j * tn, tn)

    def rows(b):
        return pl.ds(pl.multiple_of((base + b) * sub, sub), sub)

    def x_copy(b, slot):
        return pltpu.make_async_copy(x_hbm.at[rows(b)], xbuf.at[slot], in_sem.at[slot])

    def y_copy(b, slot):
        return pltpu.make_async_copy(ybuf.at[slot], y_hbm.at[rows(b), pl.ds(col, tn)], out_sem.at[slot])

    @pl.when(n > 0)
    def _():
        x_copy(0, 0).start()
        for w_ref, wb_ref in zip(w_refs, wb_refs):
            wb_ref[...] = w_ref[...].astype(BF16)

        def body(b, carry):
            slot = b % 2
            x_copy(b, slot).wait()

            @pl.when(b + 1 < n)
            def _():
                x_copy(b + 1, 1 - slot).start()

            @pl.when(b >= 2)
            def _():
                y_copy(b - 2, slot).wait()

            ybuf[slot] = compute(xbuf[slot], *[wb_ref[...] for wb_ref in wb_refs]).astype(ybuf.dtype)
            y_copy(b, slot).start()
            return carry

        lax.fori_loop(0, n, body, 0)

        @pl.when(n >= 2)
        def _():
            y_copy(n - 2, n % 2).wait()

        y_copy(n - 1, (n - 1) % 2).wait()

    @pl.when(e == pl.num_programs(1) - 1)
    def _():
        ybuf[0] = jnp.zeros(ybuf.shape[1:], ybuf.dtype)

        def fill(b, carry):
            cp = pltpu.make_async_copy(
                ybuf.at[0], y_hbm.at[pl.ds(pl.multiple_of(b * sub, sub), sub), pl.ds(col, tn)], out_sem.at[0])
            cp.start()
            cp.wait()
            return carry

        lax.fori_loop(base + n, n_blocks, fill, 0)


def _grouped_matmul(x, ws, tables, compute, moe_idx, sub, tn, out_dtype, name):
    n_rows, k = x.shape
    n_experts = ws[0].shape[1]
    n_out = ws[0].shape[-1]
    n_w = len(ws)
    kern = functools.partial(_grouped_kernel, n_w=n_w, compute=compute, sub=sub, tn=tn, n_blocks=n_rows // sub)
    w_spec = pl.BlockSpec((None, None, k, tn), lambda j, e, start, count: (moe_idx, e, 0, j))
    return pl.pallas_call(
        kern,
        grid_spec=pltpu.PrefetchScalarGridSpec(
            num_scalar_prefetch=2,
            grid=(n_out // tn, n_experts),
            in_specs=[pl.BlockSpec(memory_space=pl.ANY)] + [w_spec] * n_w,
            out_specs=pl.BlockSpec(memory_space=pl.ANY),
            scratch_shapes=([pltpu.VMEM((2, sub, k), x.dtype), pltpu.VMEM((2, sub, tn), out_dtype)]
                            + [pltpu.VMEM((k, tn), BF16)] * n_w
                            + [pltpu.SemaphoreType.DMA((2,)), pltpu.SemaphoreType.DMA((2,))])),
        out_shape=jax.ShapeDtypeStruct((n_rows, n_out), out_dtype),
        compiler_params=_params("arbitrary", "arbitrary"),
        name=name,
    )(*tables, x, *ws)


def _swiglu_gate(x, w1, w3):
    return _silu(_dot(x, w1)) * _dot(x, w3)


def _moe_experts(xg, tables, w1, w3, w2, moe_idx, sub, tf=512, tn=512):
    g = _grouped_matmul(xg, (w1, w3), tables, _swiglu_gate, moe_idx, sub, tf, BF16, "moe_gate")
    return _grouped_matmul(g, (w2,), tables, _dot, moe_idx, sub, tn, F32, "moe_down")


def _combine_kernel(pos_ref, pos_next_ref, y_hbm, route_ref, x_ref, postg_ref, mod_ref, xo_ref, buf, sem, *, tt):
    i = pl.program_id(0)
    slot = i % 2

    def row_copy(idx_ref, s, k, r):
        return pltpu.make_async_copy(y_hbm.at[pl.ds(idx_ref[0, 0, TOP_K * r + k], 1)],
                                     buf.at[s, k, pl.ds(r, 1)], sem.at[s])

    def start_all(idx_ref, s):
        def start(r, carry):
            for k in range(TOP_K):
                row_copy(idx_ref, s, k, r).start()
            return carry
        lax.fori_loop(0, tt, start, 0, unroll=4)

    @pl.when(i == 0)
    def _():
        start_all(pos_ref, 0)

    @pl.when(i + 1 < pl.num_programs(0))
    def _():
        start_all(pos_next_ref, 1 - slot)

    def wait(r, carry):
        for k in range(TOP_K):
            row_copy(pos_ref, slot, k, r).wait()
        return carry

    lax.fori_loop(0, tt, wait, 0, unroll=4)
    route = route_ref[...]
    f = route[:, 2:3] * buf[slot, 0] + route[:, 3:4] * buf[slot, 1]
    xo_ref[0] = _residual(f, x_ref[0], postg_ref[...], mod_ref, G2)


def _combine(y, pos, route, x, post_g, mod, tt=256):
    bsz, t, d = x.shape
    n = bsz * t
    tpb = t // tt
    n_tiles = n // tt
    pos3 = pos.reshape(n_tiles, 1, TOP_K * tt)
    return pl.pallas_call(
        functools.partial(_combine_kernel, tt=tt),
        grid=(n_tiles,),
        in_specs=[pl.BlockSpec((1, 1, TOP_K * tt), lambda i: (i, 0, 0), memory_space=pltpu.SMEM),
                  pl.BlockSpec((1, 1, TOP_K * tt), lambda i: (jnp.minimum(i + 1, n_tiles - 1), 0, 0),
                               memory_space=pltpu.SMEM),
                  pl.BlockSpec(memory_space=pl.ANY),
                  pl.BlockSpec((tt, LANES), lambda i: (i, 0)),
                  pl.BlockSpec((1, tt, d), lambda i: (i // tpb, i % tpb, 0)),
                  pl.BlockSpec((1, d), lambda i: (0, 0)),
                  pl.BlockSpec((1, 6, d), lambda i: (i // tpb, 0, 0))],
        out_specs=pl.BlockSpec((1, tt, d), lambda i: (i // tpb, i % tpb, 0)),
        out_shape=jax.ShapeDtypeStruct((bsz, t, d), F32),
        scratch_shapes=[pltpu.VMEM((2, TOP_K, tt, d), F32), pltpu.SemaphoreType.DMA((2,))],
        compiler_params=_params("arbitrary"),
        name="moe_combine",
    )(pos3, pos3, y, route, x, post_g.reshape(1, d), mod)


def _routing_tables(route, n_experts, sub):
    experts = route[:, :TOP_K].astype(jnp.int32).reshape(-1)
    onehot = (experts[:, None] == jnp.arange(n_experts, dtype=jnp.int32)[None, :]).astype(jnp.int32)
    csum = jnp.cumsum(onehot, axis=0)
    rank = jnp.sum(csum * onehot, axis=1) - 1
    counts = csum[-1]
    blocks = (counts + sub - 1) // sub
    block_end = jnp.cumsum(blocks)
    offs = (block_end - blocks) * sub
    pos = offs[experts] + rank
    return pos.astype(jnp.int32), ((block_end - blocks).astype(jnp.int32), blocks.astype(jnp.int32))


def _moe(h, x, router_w, w1, w3, w2, moe_idx, post_g, mod, sub=512):
    bsz, t, d = x.shape
    n = bsz * t
    n_experts = router_w.shape[-1]
    n_items = (n * TOP_K) // sub + n_experts
    hf = h.reshape(n, d)
    route = _router(hf, router_w[moe_idx])
    pos, tables = _routing_tables(route, n_experts, sub)
    xg = _dispatch(hf, _invert(pos, n_items * sub), sub)
    y = _moe_experts(xg, tables, w1, w3, w2, moe_idx, sub)
    return _combine(y, pos, route, x, post_g, mod)


def kernel(x, c, ctx, c_ctx, w_ada, b_ada, pre_mix_g, post_mix_g, pre_ffn_g, post_ffn_g, w_in, gate_w2_f, gate_b_f, gate_w2_b, gate_b_b, gla_norm_g, conv_w, conv_b, w_proj_a, w_proj_b, w_out, ffn_w1, ffn_w3, ffn_w2, router_w, moe_w1, moe_w3, moe_w2):
    bsz, seq, d = x.shape
    ctx_len = ctx.shape[1]
    depth = w_ada.shape[0]
    dk = gate_w2_f.shape[-1]
    dv = gla_norm_g.shape[-1]
    hk, hv = dk // N_HEADS, dv // N_HEADS
    off_lr = 2 * dk + dv
    n_lr = 2 * GATE_RANK
    n_qkv = 2 * dk + dv
    off_r = 0
    off_cb = off_r + dv
    off_ga = off_cb + 3 * d
    off_gb = off_ga + d
    n_gates = off_gb + d

    cc = jnp.concatenate([c, c_ctx[None, :], jnp.zeros((8 - bsz - 1, d), F32)], axis=0)
    mod_all = _ada(cc, w_ada, b_ada)

    xc = ctx
    hx = hc = None
    for layer in range(depth):
        last = layer == depth - 1
        mod_x = mod_all[layer, :bsz].reshape(bsz, 6, d)
        mod_c = jnp.broadcast_to(mod_all[layer, bsz].reshape(1, 6, d), (bsz, 6, d))

        w_qkv = w_in[layer, :, :n_qkv].astype(BF16)
        w_gates = w_in[layer, :, off_lr + n_lr:].astype(BF16)
        w_lr = jnp.zeros((d, LANES), BF16).at[:, :n_lr].set(w_in[layer, :, off_lr:off_lr + n_lr].astype(BF16))
        w2 = jnp.zeros((LANES, 2 * dk), BF16)
        w2 = w2.at[:GATE_RANK, :dk].set(gate_w2_f[layer].astype(BF16))
        w2 = w2.at[GATE_RANK:n_lr, dk:].set(gate_w2_b[layer].astype(BF16))
        gate_b = jnp.concatenate([gate_b_f[layer], gate_b_b[layer]]).reshape(1, 2 * dk)
        w_a = w_proj_a[layer].astype(BF16)
        w_b = w_proj_b[layer].astype(BF16)
        w_o = w_out[layer].astype(BF16)

        if hx is None:
            hx = _prenorm(x, pre_mix_g[layer], mod_x)
            hc = _prenorm(xc, pre_mix_g[layer], mod_c)

        hc2 = hc.reshape(bsz * ctx_len, d)
        pc = _matmul(hc2, w_qkv, n_qkv).reshape(bsz, ctx_len, n_qkv)
        la_c = _decay(hc2, w_lr, w2, gate_b).reshape(bsz, ctx_len, 2 * dk)
        s0 = jnp.zeros((bsz, N_HEADS, hv, hk), F32)
        if last:
            s_f, s_b = _gla(pc, None, la_c, s0, s0, None, dk, dv, off_r, want_out=False)
        else:
            gc = _matmul(hc2, w_gates, n_gates).reshape(bsz, ctx_len, n_gates)
            o_c, s_f, s_b = _gla(pc, gc, la_c, s0, s0, gla_norm_g[layer], dk, dv, off_r, want_out=True)

        hx2 = hx.reshape(bsz * seq, d)
        px = _matmul(hx2, w_qkv, n_qkv).reshape(bsz, seq, n_qkv)
        gx = _matmul(hx2, w_gates, n_gates).reshape(bsz, seq, n_gates)
        la_x = _decay(hx2, w_lr, w2, gate_b).reshape(bsz, seq, 2 * dk)
        o_x, _, _ = _gla(px, gx, la_x, s_f, s_b, gla_norm_g[layer], dk, dv, off_r, want_out=True)
        a_x = _gated_conv(gx, conv_w[layer], conv_b[layer], d, off_cb, seg=GRID_W)
        merged = _merge(a_x.reshape(bsz * seq, d), o_x.reshape(bsz * seq, d), w_a, w_b,
                        gx.reshape(bsz * seq, n_gates), off_ga, off_gb)
        moe_layer = layer % 2 == 1
        x, h2 = _outproj(merged, w_o, x, post_mix_g[layer], pre_ffn_g[layer], mod_x,
                         F32 if moe_layer else BF16)
        nxt = None if last else (pre_mix_g[layer + 1], mod_all[layer + 1, :bsz].reshape(bsz, 6, d))
        if moe_layer:
            x = _moe(h2, x, router_w, moe_w1, moe_w3, moe_w2, layer // 2, post_ffn_g[layer], mod_x)
            if nxt is not None:
                hx = _prenorm(x, nxt[0], nxt[1])
        else:
            f_x = _ffn(h2.reshape(bsz * seq, d), ffn_w1, ffn_w3, ffn_w2, layer // 2)
            if nxt is None:
                (x,) = _ffn_residual(f_x, x, post_ffn_g[layer], mod_x)
            else:
                x, hx = _ffn_residual(f_x, x, post_ffn_g[layer], mod_x, nxt[0], nxt[1])

        if not last:
            a_c = _gated_conv(gc, conv_w[layer], conv_b[layer], d, off_cb, seg=ctx_len, tt=ctx_len)
            merged_c = _merge(a_c.reshape(bsz * ctx_len, d), o_c.reshape(bsz * ctx_len, d), w_a, w_b,
                              gc.reshape(bsz * ctx_len, n_gates), off_ga, off_gb)
            xc, h2c = _outproj(merged_c, w_o, xc, post_mix_g[layer], pre_ffn_g[layer], mod_c,
                               F32 if moe_layer else BF16)
            mod_cn = jnp.broadcast_to(mod_all[layer + 1, bsz].reshape(1, 6, d), (bsz, 6, d))
            if moe_layer:
                xc = _moe(h2c, xc, router_w, moe_w1, moe_w3, moe_w2, layer // 2, post_ffn_g[layer], mod_c)
                hc = _prenorm(xc, pre_mix_g[layer + 1], mod_cn)
            else:
                f_c = _ffn(h2c.reshape(bsz * ctx_len, d), ffn_w1, ffn_w3, ffn_w2, layer // 2)
                xc, hc = _ffn_residual(f_c, xc, post_ffn_g[layer], mod_c, pre_mix_g[layer + 1], mod_cn)
    return x
```

```python
import functools

import jax
import jax.numpy as jnp
from jax import lax
from jax.experimental import pallas as pl
from jax.experimental.pallas import tpu as pltpu

F32 = jnp.float32
BF16 = jnp.bfloat16

EPS = 1e-6
N_HEADS = 4
GATE_RANK = 16
GATE_TAU = 16.0
GLA_CHUNK = 64
GRID_W = 64
TOP_K = 2

LANES = 128
VMEM_LIMIT = 56 * 1024 * 1024

SH1, SC1, G1, SH2, SC2, G2 = range(6)


def _params(*sem):
    return pltpu.CompilerParams(dimension_semantics=sem, vmem_limit_bytes=VMEM_LIMIT)


def _dot(a, b):
    return jnp.dot(a, b, preferred_element_type=F32)


def _dot_nt(a, b):
    return lax.dot_general(a, b, (((1,), (1,)), ((), ())), preferred_element_type=F32)


def _dot_tn(a, b):
    return lax.dot_general(a, b, (((0,), (0,)), ((), ())), preferred_element_type=F32)


def _silu(x):
    return x * jax.nn.sigmoid(x)


def _rms(x):
    return x * lax.rsqrt(jnp.mean(x * x, axis=-1, keepdims=True) + EPS)


def _ada_kernel(c_ref, w_ref, b_ref, o_ref):
    s = _silu(c_ref[...]).astype(BF16)
    o_ref[0] = _dot(s, w_ref[0].astype(BF16)) + b_ref[0]


def _ada(cc, w_ada, b_ada, tn=1024):
    depth, d, n = w_ada.shape
    rows = cc.shape[0]
    return pl.pallas_call(
        _ada_kernel,
        grid=(depth, n // tn),
        in_specs=[pl.BlockSpec((rows, d), lambda l, j: (0, 0)),
                  pl.BlockSpec((1, d, tn), lambda l, j: (l, 0, j)),
                  pl.BlockSpec((1, 1, tn), lambda l, j: (l, 0, j))],
        out_specs=pl.BlockSpec((1, rows, tn), lambda l, j: (l, 0, j)),
        out_shape=jax.ShapeDtypeStruct((depth, rows, n), F32),
        compiler_params=_params("parallel", "parallel"),
        name="ada",
    )(cc, w_ada, b_ada.reshape(depth, 1, n))


def _modulated_norm(x, g, mod_ref, shift_row, scale_row):
    h = _rms(x) * g
    return h * (1.0 + mod_ref[0, scale_row:scale_row + 1, :]) + mod_ref[0, shift_row:shift_row + 1, :]


def _prenorm_kernel(x_ref, g_ref, mod_ref, o_ref):
    o_ref[0] = _modulated_norm(x_ref[0], g_ref[...], mod_ref, SH1, SC1).astype(o_ref.dtype)


def _prenorm(x, g, mod, tt=256):
    b, t, d = x.shape
    return pl.pallas_call(
        _prenorm_kernel,
        grid=(b, t // tt),
        in_specs=[pl.BlockSpec((1, tt, d), lambda i, j: (i, j, 0)),
                  pl.BlockSpec((1, d), lambda i, j: (0, 0)),
                  pl.BlockSpec((1, 6, d), lambda i, j: (i, 0, 0))],
        out_specs=pl.BlockSpec((1, tt, d), lambda i, j: (i, j, 0)),
        out_shape=jax.ShapeDtypeStruct((b, t, d), BF16),
        compiler_params=_params("parallel", "parallel"),
        name="prenorm",
    )(x, g.reshape(1, d), mod)


def _mm_kernel(h_ref, w_ref, o_ref):
    o_ref[...] = _dot(h_ref[...], w_ref[...])


def _matmul(h, w, ncols, tm=1024, tn=1024):
    n, k = h.shape
    tm = min(tm, n)
    return pl.pallas_call(
        _mm_kernel,
        grid=(n // tm, ncols // tn),
        in_specs=[pl.BlockSpec((tm, k), lambda i, j: (i, 0)),
                  pl.BlockSpec((k, tn), lambda i, j: (0, j))],
        out_specs=pl.BlockSpec((tm, tn), lambda i, j: (i, j)),
        out_shape=jax.ShapeDtypeStruct((n, ncols), F32),
        compiler_params=_params("parallel", "parallel"),
        name="in_proj",
    )(h, w)


def _decay_kernel(h_ref, wlr_ref, w2_ref, b_ref, o_ref):
    lr = _dot(h_ref[...], wlr_ref[...]).astype(BF16)
    z = _dot(lr, w2_ref[...]) + b_ref[...]
    log_sig = jnp.minimum(z, 0.0) - jnp.log1p(jnp.exp(-jnp.abs(z)))
    o_ref[...] = log_sig / GATE_TAU


def _decay(h, w_lr, w2, bias, tm=512):
    n, k = h.shape
    r = w_lr.shape[1]
    m = w2.shape[1]
    return pl.pallas_call(
        _decay_kernel,
        grid=(n // tm,),
        in_specs=[pl.BlockSpec((tm, k), lambda i: (i, 0)),
                  pl.BlockSpec((k, r), lambda i: (0, 0)),
                  pl.BlockSpec((r, m), lambda i: (0, 0)),
                  pl.BlockSpec((1, m), lambda i: (0, 0))],
        out_specs=pl.BlockSpec((tm, m), lambda i: (i, 0)),
        out_shape=jax.ShapeDtypeStruct((n, m), F32),
        compiler_params=_params("parallel"),
        name="decay",
    )(h, w_lr, w2, bias)


def _gla_kernel(*refs, t, scale, want_out):
    if want_out:
        (q_ref, k_ref, v_ref, laf_ref, lab_ref, s0f_ref, s0b_ref, r_ref, g_ref,
         o_ref, sf_ref, sb_ref,
         of_scr, ob_scr, qtf_scr, qtb_scr, kef_scr, keb_scr, df_scr, db_scr, stf_scr, stb_scr) = refs
    else:
        (q_ref, k_ref, v_ref, laf_ref, lab_ref, s0f_ref, s0b_ref,
         sf_ref, sb_ref, kef_scr, keb_scr, df_scr, db_scr, stf_scr, stb_scr) = refs
        of_scr = ob_scr = qtf_scr = qtb_scr = None
    c_len = GLA_CHUNK
    n_chunks = t // c_len
    blk_len = min(4 * c_len, t)
    n_blocks = t // blk_len
    per_blk = blk_len // c_len
    hk = k_ref.shape[-1]
    row = lax.broadcasted_iota(jnp.int32, (blk_len, blk_len), 0)
    col = lax.broadcasted_iota(jnp.int32, (blk_len, blk_len), 1)
    same_chunk = (row // c_len) == (col // c_len)
    mask_f = same_chunk & (row >= col)
    mask_b = same_chunk & (row <= col)
    tri_f = mask_f.astype(BF16)
    tri_b = mask_b.astype(BF16)

    dirs = ((laf_ref, tri_f, mask_f, c_len - 1, qtf_scr, kef_scr, df_scr, of_scr),
            (lab_ref, tri_b, mask_b, 0, qtb_scr, keb_scr, db_scr, ob_scr))

    def prepare_body(blk, carry):
        sl = pl.ds(pl.multiple_of(blk * blk_len, blk_len), blk_len)
        k = k_ref[0, sl, :]
        cum = []
        for la_ref, tri, *_ in dirs:
            la = la_ref[0, sl, :]
            la_hi = la.astype(BF16)
            la_lo = (la - la_hi.astype(F32)).astype(BF16)
            cum.append(_dot(tri, la_hi) + _dot(tri, la_lo))
        decayed = []
        for b, (_, _, _, last_row, qt_scr, ke_scr, d_scr, _) in zip(cum, dirs):
            b_last = jnp.concatenate(
                [jnp.broadcast_to(b[i * c_len + last_row:i * c_len + last_row + 1, :], (c_len, hk))
                 for i in range(per_blk)], axis=0)
            ke_scr[sl, :] = (k * jnp.exp(b_last - b)).astype(BF16)
            d = jnp.exp(b_last)
            for i in range(per_blk):
                d_scr[pl.ds(pl.multiple_of((blk * per_blk + i) * 8, 8), 8), :] = d[i * c_len:i * c_len + 8, :]
            if want_out:
                qt = (q_ref[0, sl, :] * scale * jnp.exp(b)).astype(BF16)
                qt_scr[sl, :] = qt
                decayed.append((qt, (k * jnp.exp(-b)).astype(BF16)))
        if want_out:
            scores = [_dot_nt(qt, kt) for qt, kt in decayed]
            v = v_ref[0, sl, :].astype(BF16)
            for sc, (_, _, mask, _, _, _, _, o_scr) in zip(scores, dirs):
                o_scr[sl, :] = _dot(jnp.where(mask, sc, 0.0).astype(BF16), v)
        return carry

    lax.fori_loop(0, n_blocks, prepare_body, 0)

    stf_scr[...] = s0f_ref[0, 0]
    stb_scr[...] = s0b_ref[0, 0]

    def scan_step(c, qt_scr, ke_scr, d_scr, o_scr, st_scr):
        sl = pl.ds(pl.multiple_of(c * c_len, c_len), c_len)
        st = st_scr[...]
        if want_out:
            o_scr[sl, :] += _dot_nt(qt_scr[sl, :], st.astype(BF16))
        d = d_scr[pl.ds(pl.multiple_of(c * 8, 8), 8), :]
        st_scr[...] = d[0:1, :] * st + _dot_tn(v_ref[0, sl, :].astype(BF16), ke_scr[sl, :])

    def scan_body(c, carry):
        scan_step(c, qtf_scr, kef_scr, df_scr, of_scr, stf_scr)
        scan_step(n_chunks - 1 - c, qtb_scr, keb_scr, db_scr, ob_scr, stb_scr)
        return carry

    lax.fori_loop(0, n_chunks, scan_body, 0, unroll=2)
    sf_ref[0, 0] = stf_scr[...]
    sb_ref[0, 0] = stb_scr[...]

    if want_out:
        def epilogue(c, carry):
            sl = pl.ds(pl.multiple_of(c * c_len, c_len), c_len)
            o = _rms(of_scr[sl, :] + ob_scr[sl, :]) * g_ref[...]
            o_ref[0, sl, :] = (o * _silu(r_ref[0, sl, :])).astype(o_ref.dtype)
            return carry

        lax.fori_loop(0, n_chunks, epilogue, 0)


def _gla(p, p_r, la, s0f, s0b, gla_g, dk_total, dv_total, off_r, want_out):
    bsz, t, _ = p.shape
    hk = dk_total // N_HEADS
    hv = dv_total // N_HEADS
    nkb = dk_total // hk
    kern = functools.partial(_gla_kernel, t=t, scale=hk ** -0.5, want_out=want_out)
    n_chunks = t // GLA_CHUNK
    in_specs = [
        pl.BlockSpec((1, t, hk), lambda b, h: (b, 0, h)),
        pl.BlockSpec((1, t, hk), lambda b, h: (b, 0, nkb + h)),
        pl.BlockSpec((1, t, hv), lambda b, h: (b, 0, 2 * dk_total // hv + h)),
        pl.BlockSpec((1, t, hk), lambda b, h: (b, 0, h)),
        pl.BlockSpec((1, t, hk), lambda b, h: (b, 0, nkb + h)),
        pl.BlockSpec((1, 1, hv, hk), lambda b, h: (b, h, 0, 0)),
        pl.BlockSpec((1, 1, hv, hk), lambda b, h: (b, h, 0, 0)),
    ]
    args = [p, p, p, la, la, s0f, s0b]
    st_shape = jax.ShapeDtypeStruct((bsz, N_HEADS, hv, hk), F32)
    st_spec = pl.BlockSpec((1, 1, hv, hk), lambda b, h: (b, h, 0, 0))
    scratch = [pltpu.VMEM((t, hk), BF16), pltpu.VMEM((t, hk), BF16),
               pltpu.VMEM((n_chunks * 8, hk), F32), pltpu.VMEM((n_chunks * 8, hk), F32),
               pltpu.VMEM((hv, hk), F32), pltpu.VMEM((hv, hk), F32)]
    if want_out:
        in_specs += [pl.BlockSpec((1, t, hv), lambda b, h: (b, 0, off_r // hv + h)),
                     pl.BlockSpec((1, hv), lambda b, h: (0, h))]
        args += [p_r, gla_g.reshape(1, dv_total)]
        out_shape = (jax.ShapeDtypeStruct((bsz, t, dv_total), BF16), st_shape, st_shape)
        out_specs = (pl.BlockSpec((1, t, hv), lambda b, h: (b, 0, h)), st_spec, st_spec)
        scratch = [pltpu.VMEM((t, hv), F32), pltpu.VMEM((t, hv), F32),
                   pltpu.VMEM((t, hk), BF16), pltpu.VMEM((t, hk), BF16)] + scratch
    else:
        out_shape = (st_shape, st_shape)
        out_specs = (st_spec, st_spec)
    return pl.pallas_call(
        kern,
        grid=(bsz, N_HEADS),
        in_specs=in_specs,
        out_specs=out_specs,
        out_shape=out_shape,
        scratch_shapes=scratch,
        compiler_params=_params("parallel", "parallel"),
        name="gla",
    )(*args)


def _conv_kernel(cb_ref, cc_ref, cx_ref, w_ref, b_ref, o_ref, *, seg):
    u = cc_ref[0] * cx_ref[0]
    tt = u.shape[0]
    pos = lax.broadcasted_iota(jnp.int32, u.shape, 0) % seg
    u_prev = jnp.where(pos == 0, 0.0, pltpu.roll(u, 1, axis=0))
    u_next = jnp.where(pos == seg - 1, 0.0, pltpu.roll(u, tt - 1, axis=0))
    conv = b_ref[...] + u_prev * w_ref[0:1, :]
    conv = conv + u * w_ref[1:2, :]
    conv = conv + u_next * w_ref[2:3, :]
    o_ref[0] = (cb_ref[0] * conv).astype(o_ref.dtype)


def _gated_conv(p, conv_w, conv_b, d, off_cb, seg, tt=256):
    bsz, t, _ = p.shape
    cb = off_cb // d
    kern = functools.partial(_conv_kernel, seg=seg)
    return pl.pallas_call(
        kern,
        grid=(bsz, t // tt),
        in_specs=[pl.BlockSpec((1, tt, d), lambda b, i: (b, i, cb)),
                  pl.BlockSpec((1, tt, d), lambda b, i: (b, i, cb + 1)),
                  pl.BlockSpec((1, tt, d), lambda b, i: (b, i, cb + 2)),
                  pl.BlockSpec((3, d), lambda b, i: (0, 0)),
                  pl.BlockSpec((1, d), lambda b, i: (0, 0))],
        out_specs=pl.BlockSpec((1, tt, d), lambda b, i: (b, i, 0)),
        out_shape=jax.ShapeDtypeStruct((bsz, t, d), BF16),
        compiler_params=_params("parallel", "parallel"),
        name="gated_conv",
    )(p, p, p, conv_w, conv_b.reshape(1, d))


def _merge_kernel(a_ref, b_ref, wa_ref, wb_ref, ga_ref, gb_ref, o_ref):
    y_a = _dot(a_ref[...], wa_ref[...])
    y_b = _dot(b_ref[...], wb_ref[...])
    o_ref[...] = (jax.nn.sigmoid(ga_ref[...]) * y_a + jax.nn.sigmoid(gb_ref[...]) * y_b).astype(o_ref.dtype)


def _merge(a_in, b_in, w_a, w_b, p, off_ga, off_gb, tm=1024, tn=512):
    n, d = a_in.shape
    tm = min(tm, n)
    return pl.pallas_call(
        _merge_kernel,
        grid=(n // tm, d // tn),
        in_specs=[pl.BlockSpec((tm, d), lambda i, j: (i, 0)),
                  pl.BlockSpec((tm, d), lambda i, j: (i, 0)),
                  pl.BlockSpec((d, tn), lambda i, j: (0, j)),
                  pl.BlockSpec((d, tn), lambda i, j: (0, j)),
                  pl.BlockSpec((tm, tn), lambda i, j: (i, off_ga // tn + j)),
                  pl.BlockSpec((tm, tn), lambda i, j: (i, off_gb // tn + j))],
        out_specs=pl.BlockSpec((tm, tn), lambda i, j: (i, j)),
        out_shape=jax.ShapeDtypeStruct((n, d), BF16),
        compiler_params=_params("parallel", "parallel"),
        name="merge",
    )(a_in, b_in, w_a, w_b, p, p)


def _residual(y, x, post_g, mod_ref, gate_row):
    return x + mod_ref[0, gate_row:gate_row + 1, :] * (_rms(y) * post_g)


def _outproj_kernel(m_ref, w_ref, x_ref, postg_ref, preg_ref, mod_ref, xo_ref, h_ref):
    y = _dot(m_ref[0], w_ref[...])
    x_new = _residual(y, x_ref[0], postg_ref[...], mod_ref, G1)
    xo_ref[0] = x_new
    h_ref[0] = _modulated_norm(x_new, preg_ref[...], mod_ref, SH2, SC2).astype(h_ref.dtype)


def _outproj(merged, w_o, x, post_g, pre_g, mod, h_dtype, tm=512):
    bsz, t, d = x.shape
    tm = min(tm, t)
    return pl.pallas_call(
        _outproj_kernel,
        grid=(bsz, t // tm),
        in_specs=[pl.BlockSpec((1, tm, d), lambda b, i: (b, i, 0)),
                  pl.BlockSpec((d, d), lambda b, i: (0, 0)),
                  pl.BlockSpec((1, tm, d), lambda b, i: (b, i, 0)),
                  pl.BlockSpec((1, d), lambda b, i: (0, 0)),
                  pl.BlockSpec((1, d), lambda b, i: (0, 0)),
                  pl.BlockSpec((1, 6, d), lambda b, i: (b, 0, 0))],
        out_specs=(pl.BlockSpec((1, tm, d), lambda b, i: (b, i, 0)),
                   pl.BlockSpec((1, tm, d), lambda b, i: (b, i, 0))),
        out_shape=(jax.ShapeDtypeStruct((bsz, t, d), F32),
                   jax.ShapeDtypeStruct((bsz, t, d), h_dtype)),
        compiler_params=_params("parallel", "parallel"),
        name="out_proj",
    )(merged.reshape(bsz, t, d), w_o, x, post_g.reshape(1, d), pre_g.reshape(1, d), mod)


def _ffn_res_kernel(*refs, with_next):
    if with_next:
        f_ref, x_ref, postg_ref, mod_ref, preg_ref, modn_ref, xo_ref, h_ref = refs
    else:
        f_ref, x_ref, postg_ref, mod_ref, xo_ref = refs
    x_new = _residual(f_ref[0], x_ref[0], postg_ref[...], mod_ref, G2)
    xo_ref[0] = x_new
    if with_next:
        h_ref[0] = _modulated_norm(x_new, preg_ref[...], modn_ref, SH1, SC1).astype(h_ref.dtype)


def _ffn_residual(f, x, post_g, mod, next_pre_g=None, next_mod=None, tt=256):
    bsz, t, d = x.shape
    with_next = next_pre_g is not None
    tile = pl.BlockSpec((1, tt, d), lambda b, i: (b, i, 0))
    vec = pl.BlockSpec((1, d), lambda b, i: (0, 0))
    modspec = pl.BlockSpec((1, 6, d), lambda b, i: (b, 0, 0))
    in_specs = [tile, tile, vec, modspec]
    args = [f.reshape(bsz, t, d), x, post_g.reshape(1, d), mod]
    out_shape = [jax.ShapeDtypeStruct((bsz, t, d), F32)]
    out_specs = [tile]
    if with_next:
        in_specs += [vec, modspec]
        args += [next_pre_g.reshape(1, d), next_mod]
        out_shape.append(jax.ShapeDtypeStruct((bsz, t, d), BF16))
        out_specs.append(tile)
    return pl.pallas_call(
        functools.partial(_ffn_res_kernel, with_next=with_next),
        grid=(bsz, t // tt),
        in_specs=in_specs,
        out_specs=tuple(out_specs),
        out_shape=tuple(out_shape),
        compiler_params=_params("parallel", "parallel"),
        name="ffn_residual",
    )(*args)


def _ffn_gate_kernel(h_ref, w1_ref, w3_ref, g_ref):
    h = h_ref[...]
    a = _dot(h, w1_ref[...].astype(BF16))
    b = _dot(h, w3_ref[...].astype(BF16))
    g_ref[...] = (_silu(a) * b).astype(g_ref.dtype)


def _ffn_down_kernel(g_ref, w2_ref, o_ref):
    o_ref[...] = _dot(g_ref[...], w2_ref[...].astype(BF16))


def _ffn(h, w1, w3, w2, layer_idx, tm=1024, tf=512, tn=256):
    n, d = h.shape
    f = w1.shape[-1]
    tm = min(tm, n)
    g = pl.pallas_call(
        _ffn_gate_kernel,
        grid=(n // tm, f // tf),
        in_specs=[pl.BlockSpec((tm, d), lambda i, j: (i, 0)),
                  pl.BlockSpec((None, d, tf), lambda i, j: (layer_idx, 0, j)),
                  pl.BlockSpec((None, d, tf), lambda i, j: (layer_idx, 0, j))],
        out_specs=pl.BlockSpec((tm, tf), lambda i, j: (i, j)),
        out_shape=jax.ShapeDtypeStruct((n, f), BF16),
        compiler_params=_params("parallel", "parallel"),
        name="ffn_gate",
    )(h, w1, w3)
    return pl.pallas_call(
        _ffn_down_kernel,
        grid=(n // tm, d // tn),
        in_specs=[pl.BlockSpec((tm, f), lambda i, j: (i, 0)),
                  pl.BlockSpec((None, f, tn), lambda i, j: (layer_idx, 0, j))],
        out_specs=pl.BlockSpec((tm, tn), lambda i, j: (i, j)),
        out_shape=jax.ShapeDtypeStruct((n, d), F32),
        compiler_params=_params("parallel", "parallel"),
        name="ffn_down",
    )(g, w2)


def _router_kernel(h_ref, w_ref, o_ref, *, n_experts):
    h = h_ref[...]
    w = w_ref[...]
    h_hi = h.astype(BF16)
    h_lo = (h - h_hi.astype(F32)).astype(BF16)
    w_hi = w.astype(BF16)
    w_lo = (w - w_hi.astype(F32)).astype(BF16)
    logits = _dot(h_hi, w_hi) + (_dot(h_hi, w_lo) + _dot(h_lo, w_hi))
    lane = lax.broadcasted_iota(jnp.int32, logits.shape, 1)
    neg = -jnp.inf
    l1 = jnp.where(lane < n_experts, logits, neg)
    m1 = jnp.max(l1, axis=-1, keepdims=True)
    i1 = jnp.min(jnp.where(l1 == m1, lane, LANES), axis=-1, keepdims=True)
    l2 = jnp.where(lane == i1, neg, l1)
    m2 = jnp.max(l2, axis=-1, keepdims=True)
    i2 = jnp.min(jnp.where(l2 == m2, lane, LANES), axis=-1, keepdims=True)
    e = jnp.exp(m2 - m1)
    p1 = 1.0 / (1.0 + e)
    p2 = e / (1.0 + e)
    out = jnp.where(lane == 0, i1.astype(F32),
                    jnp.where(lane == 1, i2.astype(F32),
                              jnp.where(lane == 2, p1, jnp.where(lane == 3, p2, 0.0))))
    o_ref[...] = out


def _router(h, w_router, tm=512):
    n, d = h.shape
    n_experts = w_router.shape[1]
    w_pad = jnp.zeros((d, LANES), F32).at[:, :n_experts].set(w_router)
    return pl.pallas_call(
        functools.partial(_router_kernel, n_experts=n_experts),
        grid=(n // tm,),
        in_specs=[pl.BlockSpec((tm, d), lambda i: (i, 0)),
                  pl.BlockSpec((d, LANES), lambda i: (0, 0))],
        out_specs=pl.BlockSpec((tm, LANES), lambda i: (i, 0)),
        out_shape=jax.ShapeDtypeStruct((n, LANES), F32),
        compiler_params=_params("parallel"),
        name="router",
    )(h, w_pad)


def _invert_kernel(pos_ref, tok_ref, *, n_pairs, n_rows):
    def zero(r, carry):
        tok_ref[r] = 0
        return carry

    def scatter(p, carry):
        tok_ref[pos_ref[p]] = p // TOP_K
        return carry

    lax.fori_loop(0, n_rows, zero, 0, unroll=16)
    lax.fori_loop(0, n_pairs, scatter, 0, unroll=16)


def _invert(pos, n_rows):
    return pl.pallas_call(
        functools.partial(_invert_kernel, n_pairs=pos.shape[0], n_rows=n_rows),
        in_specs=[pl.BlockSpec(memory_space=pltpu.SMEM)],
        out_specs=pl.BlockSpec(memory_space=pltpu.SMEM),
        out_shape=jax.ShapeDtypeStruct((n_rows,), jnp.int32),
        name="moe_invert",
    )(pos)


def _dispatch_kernel(tok_ref, tok_next_ref, h_hbm, xg_ref, buf, sem, *, sub):
    i = pl.program_id(0)
    slot = i % 2

    def row_copy(idx_ref, s, r):
        return pltpu.make_async_copy(h_hbm.at[pl.ds(idx_ref[0, 0, r], 1)], buf.at[s, pl.ds(r, 1)], sem.at[s])

    def start_all(idx_ref, s):
        def start(r, carry):
            row_copy(idx_ref, s, r).start()
            return carry
        lax.fori_loop(0, sub, start, 0, unroll=8)

    @pl.when(i == 0)
    def _():
        start_all(tok_ref, 0)

    @pl.when(i + 1 < pl.num_programs(0))
    def _():
        start_all(tok_next_ref, 1 - slot)

    def wait(r, carry):
        row_copy(tok_ref, slot, r).wait()
        return carry

    lax.fori_loop(0, sub, wait, 0, unroll=8)
    xg_ref[...] = buf[slot].astype(xg_ref.dtype)


def _dispatch(h, row_tok, sub):
    n, d = h.shape
    n_rows = row_tok.shape[0]
    n_blocks = n_rows // sub
    tok3 = row_tok.reshape(n_blocks, 1, sub)
    return pl.pallas_call(
        functools.partial(_dispatch_kernel, sub=sub),
        grid=(n_blocks,),
        in_specs=[pl.BlockSpec((1, 1, sub), lambda i: (i, 0, 0), memory_space=pltpu.SMEM),
                  pl.BlockSpec((1, 1, sub), lambda i: (jnp.minimum(i + 1, n_blocks - 1), 0, 0),
                               memory_space=pltpu.SMEM),
                  pl.BlockSpec(memory_space=pl.ANY)],
        out_specs=pl.BlockSpec((sub, d), lambda i: (i, 0)),
        out_shape=jax.ShapeDtypeStruct((n_rows, d), BF16),
        scratch_shapes=[pltpu.VMEM((2, sub, d), F32), pltpu.SemaphoreType.DMA((2,))],
        compiler_params=_params("arbitrary"),
        name="moe_dispatch",
    )(tok3, tok3, h)


def _moe_gate_kernel(exp_ref, first_ref, valid_ref, x_ref, w1_ref, w3_ref, g_ref, w1b, w3b):
    i = pl.program_id(1)

    @pl.when(first_ref[i] == 1)
    def _():
        w1b[...] = w1_ref[...].astype(BF16)
        w3b[...] = w3_ref[...].astype(BF16)

    @pl.when(valid_ref[i] == 1)
    def _():
        x = x_ref[...]
        a = _dot(x, w1b[...])
        b = _dot(x, w3b[...])
        g_ref[...] = (_silu(a) * b).astype(g_ref.dtype)

    @pl.when(valid_ref[i] == 0)
    def _():
        g_ref[...] = jnp.zeros_like(g_ref)


def _moe_down_kernel(exp_ref, first_ref, valid_ref, g_ref, w2_ref, y_ref, w2b):
    i = pl.program_id(1)

    @pl.when(first_ref[i] == 1)
    def _():
        w2b[...] = w2_ref[...].astype(BF16)

    @pl.when(valid_ref[i] == 1)
    def _():
        y_ref[...] = _dot(g_ref[...], w2b[...])

    @pl.when(valid_ref[i] == 0)
    def _():
        y_ref[...] = jnp.zeros_like(y_ref)


def _moe_experts(xg, tables, w1, w3, w2, moe_idx, sub, tf=512, tn=512):
    n_rows, d = xg.shape
    f = w1.shape[-1]
    n_items = n_rows // sub
    g = pl.pallas_call(
        _moe_gate_kernel,
        grid_spec=pltpu.PrefetchScalarGridSpec(
            num_scalar_prefetch=3,
            grid=(f // tf, n_items),
            in_specs=[pl.BlockSpec((sub, d), lambda j, i, exp, first, valid: (i, 0)),
                      pl.BlockSpec((None, None, d, tf), lambda j, i, exp, first, valid: (moe_idx, exp[i], 0, j)),
                      pl.BlockSpec((None, None, d, tf), lambda j, i, exp, first, valid: (moe_idx, exp[i], 0, j))],
            out_specs=pl.BlockSpec((sub, tf), lambda j, i, exp, first, valid: (i, j)),
            scratch_shapes=[pltpu.VMEM((d, tf), BF16), pltpu.VMEM((d, tf), BF16)]),
        out_shape=jax.ShapeDtypeStruct((n_rows, f), BF16),
        compiler_params=_params("arbitrary", "arbitrary"),
        name="moe_gate",
    )(*tables, xg, w1, w3)
    return pl.pallas_call(
        _moe_down_kernel,
        grid_spec=pltpu.PrefetchScalarGridSpec(
            num_scalar_prefetch=3,
            grid=(d // tn, n_items),
            in_specs=[pl.BlockSpec((sub, f), lambda j, i, exp, first, valid: (i, 0)),
                      pl.BlockSpec((None, None, f, tn), lambda j, i, exp, first, valid: (moe_idx, exp[i], 0, j))],
            out_specs=pl.BlockSpec((sub, tn), lambda j, i, exp, first, valid: (i, j)),
            scratch_shapes=[pltpu.VMEM((f, tn), BF16)]),
        out_shape=jax.ShapeDtypeStruct((n_rows, d), F32),
        compiler_params=_params("arbitrary", "arbitrary"),
        name="moe_down",
    )(*tables, g, w2)


def _combine_kernel(pos_ref, pos_next_ref, y_hbm, route_ref, x_ref, postg_ref, mod_ref, xo_ref, buf, sem, *, tt):
    i = pl.program_id(0)
    slot = i % 2

    def row_copy(idx_ref, s, k, r):
        return pltpu.make_async_copy(y_hbm.at[pl.ds(idx_ref[0, 0, TOP_K * r + k], 1)],
                                     buf.at[s, k, pl.ds(r, 1)], sem.at[s])

    def start_all(idx_ref, s):
        def start(r, carry):
            for k in range(TOP_K):
                row_copy(idx_ref, s, k, r).start()
            return carry
        lax.fori_loop(0, tt, start, 0, unroll=4)

    @pl.when(i == 0)
    def _():
        start_all(pos_ref, 0)

    @pl.when(i + 1 < pl.num_programs(0))
    def _():
        start_all(pos_next_ref, 1 - slot)

    def wait(r, carry):
        for k in range(TOP_K):
            row_copy(pos_ref, slot, k, r).wait()
        return carry

    lax.fori_loop(0, tt, wait, 0, unroll=4)
    route = route_ref[...]
    f = route[:, 2:3] * buf[slot, 0] + route[:, 3:4] * buf[slot, 1]
    xo_ref[0] = _residual(f, x_ref[0], postg_ref[...], mod_ref, G2)


def _combine(y, pos, route, x, post_g, mod, tt=256):
    bsz, t, d = x.shape
    n = bsz * t
    tpb = t // tt
    n_tiles = n // tt
    pos3 = pos.reshape(n_tiles, 1, TOP_K * tt)
    return pl.pallas_call(
        functools.partial(_combine_kernel, tt=tt),
        grid=(n_tiles,),
        in_specs=[pl.BlockSpec((1, 1, TOP_K * tt), lambda i: (i, 0, 0), memory_space=pltpu.SMEM),
                  pl.BlockSpec((1, 1, TOP_K * tt), lambda i: (jnp.minimum(i + 1, n_tiles - 1), 0, 0),
                               memory_space=pltpu.SMEM),
                  pl.BlockSpec(memory_space=pl.ANY),
                  pl.BlockSpec((tt, LANES), lambda i: (i, 0)),
                  pl.BlockSpec((1, tt, d), lambda i: (i // tpb, i % tpb, 0)),
                  pl.BlockSpec((1, d), lambda i: (0, 0)),
                  pl.BlockSpec((1, 6, d), lambda i: (i // tpb, 0, 0))],
        out_specs=pl.BlockSpec((1, tt, d), lambda i: (i // tpb, i % tpb, 0)),
        out_shape=jax.ShapeDtypeStruct((bsz, t, d), F32),
        scratch_shapes=[pltpu.VMEM((2, TOP_K, tt, d), F32), pltpu.SemaphoreType.DMA((2,))],
        compiler_params=_params("arbitrary"),
        name="moe_combine",
    )(pos3, pos3, y, route, x, post_g.reshape(1, d), mod)


def _routing_tables(route, n_experts, sub, n_items):
    experts = route[:, :TOP_K].astype(jnp.int32).reshape(-1)
    onehot = (experts[:, None] == jnp.arange(n_experts, dtype=jnp.int32)[None, :]).astype(jnp.int32)
    csum = jnp.cumsum(onehot, axis=0)
    rank = jnp.sum(csum * onehot, axis=1) - 1
    counts = csum[-1]
    blocks = (counts + sub - 1) // sub
    block_end = jnp.cumsum(blocks)
    offs = (block_end - blocks) * sub
    pos = offs[experts] + rank
    n_valid = block_end[-1]
    item = jnp.arange(n_items, dtype=jnp.int32)
    blk = jnp.minimum(item, n_valid - 1)
    exp = jnp.minimum(jnp.sum((blk[:, None] >= block_end[None, :]).astype(jnp.int32), axis=1), n_experts - 1)
    valid = (item < n_valid).astype(jnp.int32)
    prev = jnp.concatenate([jnp.full((1,), -1, jnp.int32), exp[:-1]])
    first = ((exp != prev) & (valid == 1)).astype(jnp.int32)
    return pos.astype(jnp.int32), (exp, first, valid)


def _moe(h, x, router_w, w1, w3, w2, moe_idx, post_g, mod, sub=512):
    bsz, t, d = x.shape
    n = bsz * t
    n_experts = router_w.shape[-1]
    n_items = (n * TOP_K) // sub + n_experts
    hf = h.reshape(n, d)
    route = _router(hf, router_w[moe_idx])
    pos, tables = _routing_tables(route, n_experts, sub, n_items)
    xg = _dispatch(hf, _invert(pos, n_items * sub), sub)
    y = _moe_experts(xg, tables, w1, w3, w2, moe_idx, sub)
    return _combine(y, pos, route, x, post_g, mod)


def kernel(x, c, ctx, c_ctx, w_ada, b_ada, pre_mix_g, post_mix_g, pre_ffn_g, post_ffn_g, w_in, gate_w2_f, gate_b_f, gate_w2_b, gate_b_b, gla_norm_g, conv_w, conv_b, w_proj_a, w_proj_b, w_out, ffn_w1, ffn_w3, ffn_w2, router_w, moe_w1, moe_w3, moe_w2):
    bsz, seq, d = x.shape
    ctx_len = ctx.shape[1]
    depth = w_ada.shape[0]
    dk = gate_w2_f.shape[-1]
    dv = gla_norm_g.shape[-1]
    hk, hv = dk // N_HEADS, dv // N_HEADS
    off_lr = 2 * dk + dv
    n_lr = 2 * GATE_RANK
    n_qkv = 2 * dk + dv
    off_r = 0
    off_cb = off_r + dv
    off_ga = off_cb + 3 * d
    off_gb = off_ga + d
    n_gates = off_gb + d

    cc = jnp.concatenate([c, c_ctx[None, :], jnp.zeros((8 - bsz - 1, d), F32)], axis=0)
    mod_all = _ada(cc, w_ada, b_ada)

    xc = ctx
    hx = hc = None
    for layer in range(depth):
        last = layer == depth - 1
        mod_x = mod_all[layer, :bsz].reshape(bsz, 6, d)
        mod_c = jnp.broadcast_to(mod_all[layer, bsz].reshape(1, 6, d), (bsz, 6, d))

        w_qkv = w_in[layer, :, :n_qkv].astype(BF16)
        w_gates = w_in[layer, :, off_lr + n_lr:].astype(BF16)
        w_lr = jnp.zeros((d, LANES), BF16).at[:, :n_lr].set(w_in[layer, :, off_lr:off_lr + n_lr].astype(BF16))
        w2 = jnp.zeros((LANES, 2 * dk), BF16)
        w2 = w2.at[:GATE_RANK, :dk].set(gate_w2_f[layer].astype(BF16))
        w2 = w2.at[GATE_RANK:n_lr, dk:].set(gate_w2_b[layer].astype(BF16))
        gate_b = jnp.concatenate([gate_b_f[layer], gate_b_b[layer]]).reshape(1, 2 * dk)
        w_a = w_proj_a[layer].astype(BF16)
        w_b = w_proj_b[layer].astype(BF16)
        w_o = w_out[layer].astype(BF16)

        if hx is None:
            hx = _prenorm(x, pre_mix_g[layer], mod_x)
            hc = _prenorm(xc, pre_mix_g[layer], mod_c)

        hc2 = hc.reshape(bsz * ctx_len, d)
        pc = _matmul(hc2, w_qkv, n_qkv).reshape(bsz, ctx_len, n_qkv)
        la_c = _decay(hc2, w_lr, w2, gate_b).reshape(bsz, ctx_len, 2 * dk)
        s0 = jnp.zeros((bsz, N_HEADS, hv, hk), F32)
        if last:
            s_f, s_b = _gla(pc, None, la_c, s0, s0, None, dk, dv, off_r, want_out=False)
        else:
            gc = _matmul(hc2, w_gates, n_gates).reshape(bsz, ctx_len, n_gates)
            o_c, s_f, s_b = _gla(pc, gc, la_c, s0, s0, gla_norm_g[layer], dk, dv, off_r, want_out=True)

        hx2 = hx.reshape(bsz * seq, d)
        px = _matmul(hx2, w_qkv, n_qkv).reshape(bsz, seq, n_qkv)
        gx = _matmul(hx2, w_gates, n_gates).reshape(bsz, seq, n_gates)
        la_x = _decay(hx2, w_lr, w2, gate_b).reshape(bsz, seq, 2 * dk)
        o_x, _, _ = _gla(px, gx, la_x, s_f, s_b, gla_norm_g[layer], dk, dv, off_r, want_out=True)
        a_x = _gated_conv(gx, conv_w[layer], conv_b[layer], d, off_cb, seg=GRID_W)
        merged = _merge(a_x.reshape(bsz * seq, d), o_x.reshape(bsz * seq, d), w_a, w_b,
                        gx.reshape(bsz * seq, n_gates), off_ga, off_gb)
        moe_layer = layer % 2 == 1
        x, h2 = _outproj(merged, w_o, x, post_mix_g[layer], pre_ffn_g[layer], mod_x,
                         F32 if moe_layer else BF16)
        nxt = None if last else (pre_mix_g[layer + 1], mod_all[layer + 1, :bsz].reshape(bsz, 6, d))
        if moe_layer:
            x = _moe(h2, x, router_w, moe_w1, moe_w3, moe_w2, layer // 2, post_ffn_g[layer], mod_x)
            if nxt is not None:
                hx = _prenorm(x, nxt[0], nxt[1])
        else:
            f_x = _ffn(h2.reshape(bsz * seq, d), ffn_w1, ffn_w3, ffn_w2, layer // 2)
            if nxt is None:
                (x,) = _ffn_residual(f_x, x, post_ffn_g[layer], mod_x)
            else:
                x, hx = _ffn_residual(f_x, x, post_ffn_g[layer], mod_x, nxt[0], nxt[1])

        if not last:
            a_c = _gated_conv(gc, conv_w[layer], conv_b[layer], d, off_cb, seg=ctx_len, tt=ctx_len)
            merged_c = _merge(a_c.reshape(bsz * ctx_len, d), o_c.reshape(bsz * ctx_len, d), w_a, w_b,
                              gc.reshape(bsz * ctx_len, n_gates), off_ga, off_gb)
            xc, h2c = _outproj(merged_c, w_o, xc, post_mix_g[layer], pre_ffn_g[layer], mod_c,
                               F32 if moe_layer else BF16)
            mod_cn = jnp.broadcast_to(mod_all[layer + 1, bsz].reshape(1, 6, d), (bsz, 6, d))
            if moe_layer:
                xc = _moe(h2c, xc, router_w, moe_w1, moe_w3, moe_w2, layer // 2, post_ffn_g[layer], mod_c)
                hc = _prenorm(xc, pre_mix_g[layer + 1], mod_cn)
            else:
                f_c = _ffn(h2c.reshape(bsz * ctx_len, d), ffn_w1, ffn_w3, ffn_w2, layer // 2)
                xc, hc = _ffn_residual(f_c, xc, post_ffn_g[layer], mod_c, pre_mix_g[layer + 1], mod_cn)
    return x
```
